```python
import numpy as np
import jax
import jax.numpy as jnp
from jax import lax

D_MODEL = 4096
BATCH = 2
SEQ = 4096
DEPTH = 2

GRID_W = 64
CTX_LEN = 256
CHUNK = 64
EPS = 1e-6
ROPE_BASE = 10000.0
GLA_HEADS = 8
GLA_DK = 64
GLA_DV = 128
GLA_RANK = 16
GLA_TAU = 16.0
ML_HEADS = 8
ML_DH = 128
ML_CONV = 3
NA_HEADS = 16
NA_DH = 128
NA_KH = 8
NA_KW = 16
N_BRANCH = 3

A_QK = GLA_HEADS * GLA_DK
A_V = GLA_HEADS * GLA_DV
B_W = ML_HEADS * ML_DH
C_W = NA_HEADS * NA_DH
IN_SPLITS = (A_QK, A_QK, A_V, A_V, 2 * GLA_RANK, B_W, B_W, B_W, B_W, B_W, 4 * ML_HEADS, C_W, C_W, C_W, C_W)
D_IN = 2 * A_QK + 2 * A_V + 2 * GLA_RANK + 5 * B_W + 4 * ML_HEADS + 4 * C_W

kernel_name = 'hybrid_gla_mlstm_natten_dit'


def rmsnorm(x, g):
    xf = x.astype(jnp.float32)
    xf = xf * lax.rsqrt(jnp.mean(xf * xf, axis=-1, keepdims=True) + EPS)
    return xf.astype(x.dtype) * g


def split_cols(p):
    return jnp.split(p, np.cumsum(IN_SPLITS)[:-1].tolist(), axis=-1)


def flip_t(t):
    return jnp.flip(t, axis=1)


def to_chunks(t):
    b, l = t.shape[:2]
    t = t.reshape(b, l // CHUNK, CHUNK, *t.shape[2:])
    return jnp.moveaxis(t, 1, 0)


def from_chunks(t):
    t = jnp.moveaxis(t, 0, 1)
    return t.reshape(t.shape[0], t.shape[1] * t.shape[2], *t.shape[3:])


def rope_1d(x, pos):
    nf = x.shape[-1] // 2
    inv = ROPE_BASE ** (-jnp.arange(nf, dtype=jnp.float32) / nf)
    ang = pos.astype(jnp.float32)[:, None] * inv[None, :]
    cos = jnp.cos(ang)[None, :, None, :]
    sin = jnp.sin(ang)[None, :, None, :]
    xf = x.astype(jnp.float32)
    x1, x2 = xf[..., :nf], xf[..., nf:]
    return jnp.concatenate([x1 * cos - x2 * sin, x1 * sin + x2 * cos], axis=-1).astype(x.dtype)


def axial_rope(x):
    t = jnp.arange(x.shape[1])
    half = x.shape[-1] // 2
    return jnp.concatenate([rope_1d(x[..., :half], t // GRID_W), rope_1d(x[..., half:], t % GRID_W)], axis=-1)


def gla_scan(q, k, v, log_a, s0):
    dt = v.dtype
    mask = jnp.tril(jnp.ones((CHUNK, CHUNK), dtype=bool))[None, :, :, None, None]

    def step(S, inp):
        qc, kc, vc, ac = inp
        b = jnp.cumsum(ac, axis=1)
        diff = jnp.where(mask, b[:, :, None] - b[:, None, :], -jnp.inf)
        att = jnp.einsum('bthk,bshk,btshk->bhts', qc, kc, jnp.exp(diff))
        o = jnp.einsum('bhts,bshv->bthv', att, vc) + jnp.einsum('bthk,bhkv->bthv', qc * jnp.exp(b), S)
        b_last = b[:, -1]
        S = jnp.exp(b_last)[..., None] * S + jnp.einsum('bshk,bshv->bhkv', kc * jnp.exp(b_last[:, None] - b), vc)
        return S, o

    xs = tuple(to_chunks(t.astype(jnp.float32)) for t in (q, k, v, log_a))
    S, o = lax.scan(step, s0, xs)
    return from_chunks(o).astype(dt), S


def mlstm_scan(q, k, v, i_pre, log_f, state):
    dt = v.dtype
    mask = jnp.tril(jnp.ones((CHUNK, CHUNK), dtype=bool))[None, :, :, None]

    def step(carry, inp):
        Cm, nm, mm = carry
        qc, kc, vc, ic, fc = inp
        b = jnp.cumsum(fc, axis=1)
        d_log = jnp.where(mask, b[:, :, None] - b[:, None, :] + ic[:, None, :], -jnp.inf)
        inter_log = b + mm[:, None]
        m = jnp.maximum(inter_log, jnp.max(d_log, axis=2))
        w_inter = jnp.exp(inter_log - m)
        qk = jnp.einsum('bthd,bshd->btsh', qc, kc) * jnp.exp(d_log - m[:, :, None])
        num = jnp.einsum('btsh,bshv->bthv', qk, vc) + w_inter[..., None] * jnp.einsum('bthd,bhdv->bthv', qc, Cm)
        den = jnp.sum(qk, axis=2) + w_inter * jnp.einsum('bthd,bhd->bth', qc, nm)
        h = num / jnp.maximum(jnp.abs(den), jnp.exp(-m))[..., None]
        b_last = b[:, -1]
        log_w = b_last[:, None] - b + ic
        m_new = jnp.maximum(b_last + mm, jnp.max(log_w, axis=1))
        decay = jnp.exp(b_last + mm - m_new)
        w = jnp.exp(log_w - m_new[:, None])
        Cm = decay[..., None, None] * Cm + jnp.einsum('bsh,bshd,bshv->bhdv', w, kc, vc)
        nm = decay[..., None] * nm + jnp.einsum('bsh,bshd->bhd', w, kc)
        return (Cm, nm, m_new), h

    xs = tuple(to_chunks(t.astype(jnp.float32)) for t in (q, k, v, i_pre, log_f))
    st, h = lax.scan(step, state, xs)
    return from_chunks(h).astype(dt), st


def bidirectional(scan_fn, lat_f, lat_b, ctx_f, ctx_b, init):
    yc_f, st_f = scan_fn(*ctx_f, init)
    yc_b, st_b = scan_fn(*map(flip_t, ctx_b), init)
    y_f, _ = scan_fn(*lat_f, st_f)
    y_b, _ = scan_fn(*map(flip_t, lat_b), st_b)
    return y_f + flip_t(y_b), yc_f + flip_t(yc_b)


def centred_dwconv(x, w, bias):
    pad = (ML_CONV - 1) // 2
    L = x.shape[1]
    xp = jnp.pad(x, ((0, 0), (pad, pad), (0, 0)))
    return sum(xp[:, j:j + L] * w[j] for j in range(ML_CONV)) + bias


def neighbourhood_attention(q, k, v, k_ctx, v_ctx, rpb):
    B_, L, H, d = q.shape
    rows = L // GRID_W
    kh, kw = min(NA_KH, rows), NA_KW
    qg = q.reshape(B_, rows, GRID_W, H, d)
    kg = k.reshape(B_, rows, GRID_W, H, d)
    vg = v.reshape(B_, rows, GRID_W, H, d)
    cols = jnp.arange(GRID_W)
    col_idx = jnp.clip(cols - kw // 2, 0, GRID_W - kw)[:, None] + jnp.arange(kw)[None, :]
    dc = col_idx - cols[:, None]
    scale = d ** -0.5

    def row_block(r):
        rs = jnp.clip(r - kh // 2, 0, rows - kh)
        qr = lax.dynamic_index_in_dim(qg, r, axis=1, keepdims=False)
        k_win = lax.dynamic_slice_in_dim(kg, rs, kh, axis=1)[:, :, col_idx]
        v_win = lax.dynamic_slice_in_dim(vg, rs, kh, axis=1)[:, :, col_idx]
        dr = rs + jnp.arange(kh) - r
        bias = rpb[:, dr[:, None, None] + NA_KH - 1, dc[None] + NA_KW - 1]
        s_win = jnp.einsum('bwhd,bawkhd->bhwak', qr, k_win) * scale + jnp.transpose(bias, (0, 2, 1, 3))[None]
        s_win = s_win.reshape(B_, H, GRID_W, kh * kw)
        s_ctx = jnp.einsum('bwhd,bchd->bhwc', qr, k_ctx) * scale
        p = jax.nn.softmax(jnp.concatenate([s_win, s_ctx], axis=-1).astype(jnp.float32), axis=-1).astype(v.dtype)
        p_win = p[..., :kh * kw].reshape(B_, H, GRID_W, kh, kw)
        p_ctx = p[..., kh * kw:]
        return jnp.einsum('bhwak,bawkhd->bwhd', p_win, v_win) + jnp.einsum('bhwc,bchd->bwhd', p_ctx, v_ctx)

    out = lax.map(row_block, jnp.arange(rows))
    return jnp.moveaxis(out, 0, 1).reshape(B_, L, H * d)


def ctx_attention(q, k, v):
    s = jnp.einsum('bqhd,bkhd->bhqk', q, k) * (q.shape[-1] ** -0.5)
    p = jax.nn.softmax(s.astype(jnp.float32), axis=-1).astype(v.dtype)
    return jnp.einsum('bhqk,bkhd->bqhd', p, v).reshape(q.shape[0], q.shape[1], -1)


def hybrid_mixer(h, hc, w_in, w_alpha2, b_alpha, gla_norm_g, b_gates, conv_w, conv_b, rpb,
                 w_merge, b_merge, w_proj_a, w_proj_b, w_proj_c, w_out, ctx_out):
    B_, L, _ = h.shape
    Lc = hc.shape[1]
    lat = split_cols(h @ w_in)
    ctx = split_cols(hc @ w_in)

    def gla_inputs(p, n, rope):
        qa, ka, va, za, aa = p[0:5]
        q = qa.reshape(B_, n, GLA_HEADS, GLA_DK)
        k = ka.reshape(B_, n, GLA_HEADS, GLA_DK)
        if rope:
            q, k = axial_rope(q), axial_rope(k)
        q = q * GLA_DK ** -0.5
        v = va.reshape(B_, n, GLA_HEADS, GLA_DV)
        la = [(jax.nn.log_sigmoid((aa[..., r * GLA_RANK:(r + 1) * GLA_RANK] @ w_alpha2[r] + b_alpha[r]).astype(jnp.float32)) / GLA_TAU).reshape(B_, n, GLA_HEADS, GLA_DK) for r in range(2)]
        return (q, k, v, la[0]), (q, k, v, la[1]), za

    a_f, a_b, za = gla_inputs(lat, L, True)
    ac_f, ac_b, za_c = gla_inputs(ctx, Lc, False)
    s0 = jnp.zeros((B_, GLA_HEADS, GLA_DK, GLA_DV), jnp.float32)
    oa, oa_c = bidirectional(gla_scan, a_f, a_b, ac_f, ac_b, s0)

    def ml_inputs(p, n):
        qb, kb, vb, zb, ob, gb = p[5:11]
        q = jax.nn.silu(centred_dwconv(qb, conv_w[:, :B_W], conv_b[:B_W])).reshape(B_, n, ML_HEADS, ML_DH)
        k = jax.nn.silu(centred_dwconv(kb, conv_w[:, B_W:], conv_b[B_W:])).reshape(B_, n, ML_HEADS, ML_DH) * ML_DH ** -0.5
        v = vb.reshape(B_, n, ML_HEADS, ML_DH)
        g = (gb.reshape(B_, n, 4, ML_HEADS) + b_gates).astype(jnp.float32)
        fwd = (q, k, v, g[:, :, 0], jax.nn.log_sigmoid(g[:, :, 1]))
        bwd = (q, k, v, g[:, :, 2], jax.nn.log_sigmoid(g[:, :, 3]))
        return fwd, bwd, zb, ob

    b_f, b_b, zb, ob = ml_inputs(lat, L)
    bc_f, bc_b, zb_c, ob_c = ml_inputs(ctx, Lc)
    st0 = (jnp.zeros((B_, ML_HEADS, ML_DH, ML_DH), jnp.float32),
           jnp.zeros((B_, ML_HEADS, ML_DH), jnp.float32),
           jnp.zeros((B_, ML_HEADS), jnp.float32))
    hb, hb_c = bidirectional(mlstm_scan, b_f, b_b, bc_f, bc_b, st0)

    heads = lambda t, n: t.reshape(B_, n, NA_HEADS, NA_DH)
    qn, kn, vn, zn = lat[11:15]
    qn_c, kn_c, vn_c, zn_c = ctx[11:15]
    k_ctx, v_ctx = heads(kn_c, Lc), heads(vn_c, Lc)
    on = neighbourhood_attention(heads(qn, L), heads(kn, L), heads(vn, L), k_ctx, v_ctx, rpb)

    def merge(hh, ya, yb, yc):
        ga, gb_, gc = jnp.split(jax.nn.sigmoid(hh @ w_merge + b_merge), N_BRANCH, axis=-1)
        return (ga * (ya @ w_proj_a) + gb_ * (yb @ w_proj_b) + gc * (yc @ w_proj_c)) @ w_out

    g_norm = gla_norm_g.reshape(GLA_HEADS, GLA_DV)
    ya = rmsnorm(oa, g_norm).reshape(B_, L, A_V) * jax.nn.silu(za)
    yb = jax.nn.sigmoid(ob) * hb.reshape(B_, L, B_W) * jax.nn.silu(zb)
    yc = on * jax.nn.silu(zn)
    out = merge(h, ya, yb, yc)
    if not ctx_out:
        return out, None
    ya_c = rmsnorm(oa_c, g_norm).reshape(B_, Lc, A_V) * jax.nn.silu(za_c)
    yb_c = jax.nn.sigmoid(ob_c) * hb_c.reshape(B_, Lc, B_W) * jax.nn.silu(zb_c)
    yc_c = ctx_attention(heads(qn_c, Lc), k_ctx, v_ctx) * jax.nn.silu(zn_c)
    return out, merge(hc, ya_c, yb_c, yc_c)


def setup_inputs(seed: int = 0) -> dict:
    key = jax.random.key(seed)
    ks = jax.random.split(key, 24)
    f32 = jnp.float32
    nrm = lambda k, shape, s: jax.random.normal(k, shape, f32) * s
    f_bias = jnp.linspace(3.0, 6.0, ML_HEADS, dtype=f32)
    gate_sel = jnp.array([0.0, 1.0, 0.0, 1.0], f32)[:, None]
    return {
        'x': nrm(ks[0], (BATCH, SEQ, D_MODEL), 1.0),
        'c': nrm(ks[1], (BATCH, D_MODEL), 1.0),
        'ctx': nrm(ks[2], (BATCH, CTX_LEN, D_MODEL), 1.0),
        'c_ctx': nrm(ks[3], (D_MODEL,), 1.0),
        'w_mod': nrm(ks[4], (DEPTH, D_MODEL, 3 * D_MODEL), D_MODEL ** -0.5),
        'b_mod': nrm(ks[5], (DEPTH, 3 * D_MODEL), 0.02),
        'norm_g': 1.0 + nrm(ks[6], (DEPTH, D_MODEL), 0.02),
        'w_in': nrm(ks[7], (DEPTH, D_MODEL, D_IN), D_MODEL ** -0.5),
        'w_alpha2': nrm(ks[8], (DEPTH, 2, GLA_RANK, A_QK), GLA_RANK ** -0.5),
        'b_alpha': nrm(ks[9], (DEPTH, 2, A_QK), 0.1),
        'gla_norm_g': 1.0 + nrm(ks[10], (DEPTH, A_V), 0.02),
        'b_gates': nrm(ks[11], (DEPTH, 4, ML_HEADS), 0.1) + gate_sel * f_bias[None, :],
        'conv_w': nrm(ks[12], (DEPTH, ML_CONV, 2 * B_W), ML_CONV ** -0.5),
        'conv_b': nrm(ks[13], (DEPTH, 2 * B_W), 0.02),
        'rpb': nrm(ks[14], (DEPTH, NA_HEADS, 2 * NA_KH - 1, 2 * NA_KW - 1), 0.1),
        'w_merge': nrm(ks[15], (DEPTH, D_MODEL, N_BRANCH * D_MODEL), D_MODEL ** -0.5),
        'b_merge': nrm(ks[16], (DEPTH, N_BRANCH * D_MODEL), 0.02),
        'w_proj_a': nrm(ks[17], (DEPTH, A_V, D_MODEL), A_V ** -0.5),
        'w_proj_b': nrm(ks[18], (DEPTH, B_W, D_MODEL), B_W ** -0.5),
        'w_proj_c': nrm(ks[19], (DEPTH, C_W, D_MODEL), C_W ** -0.5),
        'w_out': nrm(ks[20], (DEPTH, D_MODEL, D_MODEL), D_MODEL ** -0.5),
        'final_g': 1.0 + nrm(ks[21], (D_MODEL,), 0.02),
    }


def reference(x, c, ctx, c_ctx, w_mod, b_mod, norm_g, w_in, w_alpha2, b_alpha, gla_norm_g, b_gates,
              conv_w, conv_b, rpb, w_merge, b_merge, w_proj_a, w_proj_b, w_proj_c, w_out, final_g):
    xl, xc = x, ctx
    for l in range(DEPTH):
        last = l == DEPTH - 1
        mod = jax.nn.silu(c) @ w_mod[l] + b_mod[l]
        mod_c = jax.nn.silu(c_ctx) @ w_mod[l] + b_mod[l]
        shift, scale, gate = jnp.split(mod[:, None, :], 3, axis=-1)
        shift_c, scale_c, gate_c = jnp.split(mod_c, 3, axis=-1)
        h = rmsnorm(xl, norm_g[l]) * (1 + scale) + shift
        hc = rmsnorm(xc, norm_g[l]) * (1 + scale_c) + shift_c
        out, out_c = hybrid_mixer(h, hc, w_in[l], w_alpha2[l], b_alpha[l], gla_norm_g[l], b_gates[l],
                                  conv_w[l], conv_b[l], rpb[l], w_merge[l], b_merge[l],
                                  w_proj_a[l], w_proj_b[l], w_proj_c[l], w_out[l], not last)
        xl = xl + gate * out
        if not last:
            xc = xc + gate_c * out_c
    return rmsnorm(xl, final_g)
```

```python
import functools

import numpy as np
import jax
import jax.numpy as jnp
from jax import lax
from jax.experimental import pallas as pl
from jax.experimental.pallas import tpu as pltpu

F32 = jnp.float32
BF16 = jnp.bfloat16
HI = lax.Precision.HIGHEST

GRID_W = 64
EPS = 1e-6
ROPE_BASE = 10000.0
GLA_HEADS, GLA_DK, GLA_DV, GLA_RANK, GLA_TAU = 8, 64, 128, 16, 16.0
ML_HEADS, ML_DH = 8, 128
NA_HEADS, NA_DH, NA_KH, NA_KW = 16, 128, 8, 16
A_QK = GLA_HEADS * GLA_DK
A_V = GLA_HEADS * GLA_DV
B_W = ML_HEADS * ML_DH
C_W = NA_HEADS * NA_DH
LANE = 128
GLA_PAD = GLA_HEADS * LANE
GLA_CHUNK = 64
GLA_SUB = 16
ML_CHUNK = 128
NEG = -1e30
OFF_AA = 2 * A_QK + 2 * A_V
OFF_B = OFF_AA + 2 * GLA_RANK
OFF_GATES = OFF_B + 5 * B_W
OFF_C = OFF_GATES + 4 * ML_HEADS
VMEM_LIMIT = 52 * 1024 * 1024


def _cp(*sem):
    return pltpu.CompilerParams(dimension_semantics=sem, vmem_limit_bytes=VMEM_LIMIT)


def _nt(a, b):
    return lax.dot_general(a, b, (((1,), (1,)), ((), ())), preferred_element_type=F32)


def _tn(a, b):
    return lax.dot_general(a, b, (((0,), (0,)), ((), ())), preferred_element_type=F32)


def _log_sigmoid(x):
    return jnp.minimum(x, 0.0) - jnp.log1p(jnp.exp(-jnp.abs(x)))


def _silu(x):
    return x * jax.nn.sigmoid(x)


def _pick(n, cands):
    for c in cands:
        if n % c == 0:
            return c
    raise ValueError(f"no tile for {n}")


def _mod_kernel(s_ref, w_ref, b_ref, o_ref, *, kc):
    s = _silu(s_ref[...]).astype(BF16)
    d = s.shape[1]
    acc = jnp.zeros(o_ref.shape[1:], F32)
    for k0 in range(0, d, kc):
        acc = acc + jnp.dot(s[:, k0:k0 + kc], w_ref[0, k0:k0 + kc, :].astype(BF16),
                            preferred_element_type=F32)
    o_ref[0] = acc + b_ref[0]


def _modulation(cc, w_mod, b_mod):
    depth, d, n = w_mod.shape
    tn = _pick(n, (512, 256, 128))
    kc = _pick(d, (512, 256, 128))
    return pl.pallas_call(
        functools.partial(_mod_kernel, kc=kc),
        grid=(depth, n // tn),
        in_specs=[pl.BlockSpec((8, d), lambda l, j: (0, 0)),
                  pl.BlockSpec((1, d, tn), lambda l, j: (l, 0, j)),
                  pl.BlockSpec((1, 1, tn), lambda l, j: (l, 0, j))],
        out_specs=pl.BlockSpec((1, 8, tn), lambda l, j: (l, 0, j)),
        out_shape=jax.ShapeDtypeStruct((depth, 8, n), F32),
        compiler_params=_cp("arbitrary", "arbitrary"),
        name="modulation",
    )(cc, w_mod, b_mod.reshape(depth, 1, n))


def _norm_kernel(x_ref, g_ref, sh_ref, sc_ref, ws_ref, h_ref, sm_ref):
    x = x_ref[...]
    xn = x * lax.rsqrt(jnp.mean(x * x, axis=1, keepdims=True) + EPS)
    h = xn * g_ref[...] * (1.0 + sc_ref[0]) + sh_ref[0]
    h_ref[...] = h.astype(BF16)
    sm_ref[...] = jnp.dot(h, ws_ref[...], precision=HI, preferred_element_type=F32)


def _norm_mod(xall, g, shift, scale, w_small, tr, group_of):
    r, d = xall.shape
    return pl.pallas_call(
        _norm_kernel,
        grid=(r // tr,),
        in_specs=[pl.BlockSpec((tr, d), lambda i: (i, 0)),
                  pl.BlockSpec((1, d), lambda i: (0, 0)),
                  pl.BlockSpec((1, 1, d), lambda i: (group_of(i), 0, 0)),
                  pl.BlockSpec((1, 1, d), lambda i: (group_of(i), 0, 0)),
                  pl.BlockSpec((d, LANE), lambda i: (0, 0))],
        out_specs=[pl.BlockSpec((tr, d), lambda i: (i, 0)),
                   pl.BlockSpec((tr, LANE), lambda i: (i, 0))],
        out_shape=[jax.ShapeDtypeStruct((r, d), BF16),
                   jax.ShapeDtypeStruct((r, LANE), F32)],
        compiler_params=_cp("arbitrary"),
        name="norm_mod",
    )(xall, g.reshape(1, d), shift, scale, w_small)


def _final_norm_kernel(x_ref, g_ref, o_ref):
    x = x_ref[...]
    o_ref[...] = x * lax.rsqrt(jnp.mean(x * x, axis=1, keepdims=True) + EPS) * g_ref[...]


def _final_norm(x, g, n_rows, tr):
    d = x.shape[1]
    return pl.pallas_call(
        _final_norm_kernel,
        grid=(n_rows // tr,),
        in_specs=[pl.BlockSpec((tr, d), lambda i: (i, 0)),
                  pl.BlockSpec((1, d), lambda i: (0, 0))],
        out_specs=pl.BlockSpec((tr, d), lambda i: (i, 0)),
        out_shape=jax.ShapeDtypeStruct((n_rows, d), F32),
        compiler_params=_cp("arbitrary"),
        name="final_norm",
    )(x, g.reshape(1, d))


def _cast_weight(w_ref, wb_ref):
    k = w_ref.shape[0]
    kc = _pick(k, (512, 256, 128))
    for k0 in range(0, k, kc):
        wb_ref[k0:k0 + kc, :] = w_ref[k0:k0 + kc, :].astype(BF16)


def _mm_kernel(x_ref, w_ref, b_ref, o_ref, wb_ref, *, act):
    @pl.when(pl.program_id(1) == 0)
    def _():
        _cast_weight(w_ref, wb_ref)

    acc = jnp.dot(x_ref[...], wb_ref[...], preferred_element_type=F32)
    if act == "sigmoid_bias":
        acc = jax.nn.sigmoid(acc + b_ref[...])
    o_ref[...] = acc.astype(o_ref.dtype)


def _mm(x, w, n_rows, tm, bias=None, act="none", name="mm"):
    k, n = w.shape
    tn = _pick(n, (512, 256, 128))
    if bias is None:
        bias = jnp.zeros((1, n), F32)
    return pl.pallas_call(
        functools.partial(_mm_kernel, act=act),
        grid=(n // tn, n_rows // tm),
        in_specs=[pl.BlockSpec((tm, k), lambda j, i: (i, 0)),
                  pl.BlockSpec((k, tn), lambda j, i: (0, j)),
                  pl.BlockSpec((1, tn), lambda j, i: (0, j))],
        out_specs=pl.BlockSpec((tm, tn), lambda j, i: (i, j)),
        out_shape=jax.ShapeDtypeStruct((n_rows, n), BF16),
        scratch_shapes=[pltpu.VMEM((k, tn), BF16)],
        compiler_params=_cp("arbitrary", "arbitrary"),
        name=name,
    )(x, w, bias.reshape(1, n))


def _proj_kernel(ya_ref, yb_ref, yc_ref, wa_ref, wb_ref, wc_ref, ga_ref, gb_ref, gc_ref,
                 o_ref, wa_s, wb_s, wc_s):
    @pl.when(pl.program_id(1) == 0)
    def _():
        _cast_weight(wa_ref, wa_s)
        _cast_weight(wb_ref, wb_s)
        _cast_weight(wc_ref, wc_s)

    u = ga_ref[...].astype(F32) * jnp.dot(ya_ref[...], wa_s[...], preferred_element_type=F32)
    u = u + gb_ref[...].astype(F32) * jnp.dot(yb_ref[...], wb_s[...], preferred_element_type=F32)
    u = u + gc_ref[...].astype(F32) * jnp.dot(yc_ref[...], wc_s[...], preferred_element_type=F32)
    o_ref[...] = u.astype(o_ref.dtype)


def _proj_merge(ya, yb, yc, wa, wb, wc, gates, n_rows, tm):
    d = wa.shape[1]
    tn = _pick(d, (512, 256, 128))
    nj = d // tn
    row = lambda j, i: (i, 0)
    col = lambda j, i: (0, j)
    return pl.pallas_call(
        _proj_kernel,
        grid=(nj, n_rows // tm),
        in_specs=[pl.BlockSpec((tm, ya.shape[1]), row),
                  pl.BlockSpec((tm, yb.shape[1]), row),
                  pl.BlockSpec((tm, yc.shape[1]), row),
                  pl.BlockSpec((wa.shape[0], tn), col),
                  pl.BlockSpec((wb.shape[0], tn), col),
                  pl.BlockSpec((wc.shape[0], tn), col),
                  pl.BlockSpec((tm, tn), lambda j, i: (i, j)),
                  pl.BlockSpec((tm, tn), lambda j, i: (i, nj + j)),
                  pl.BlockSpec((tm, tn), lambda j, i: (i, 2 * nj + j))],
        out_specs=pl.BlockSpec((tm, tn), lambda j, i: (i, j)),
        out_shape=jax.ShapeDtypeStruct((n_rows, d), BF16),
        scratch_shapes=[pltpu.VMEM((wa.shape[0], tn), BF16),
                        pltpu.VMEM((wb.shape[0], tn), BF16),
                        pltpu.VMEM((wc.shape[0], tn), BF16)],
        compiler_params=_cp("arbitrary", "arbitrary"),
        name="proj_merge",
    )(ya, yb, yc, wa, wb, wc, gates, gates, gates)


def _out_kernel(u_ref, w_ref, x_ref, g_ref, o_ref, wb_ref):
    @pl.when(pl.program_id(1) == 0)
    def _():
        _cast_weight(w_ref, wb_ref)

    acc = jnp.dot(u_ref[...], wb_ref[...], preferred_element_type=F32)
    o_ref[...] = x_ref[...] + g_ref[0] * acc


def _out_proj(u, w, xall, gate, n_rows, tm, group_of_m):
    k, n = w.shape
    tn = _pick(n, (512, 256, 128))
    return pl.pallas_call(
        _out_kernel,
        grid=(n // tn, n_rows // tm),
        in_specs=[pl.BlockSpec((tm, k), lambda j, i: (i, 0)),
                  pl.BlockSpec((k, tn), lambda j, i: (0, j)),
                  pl.BlockSpec((tm, tn), lambda j, i: (i, j)),
                  pl.BlockSpec((1, 1, tn), lambda j, i: (group_of_m(i), 0, j))],
        out_specs=pl.BlockSpec((tm, tn), lambda j, i: (i, j)),
        out_shape=jax.ShapeDtypeStruct((n_rows, n), F32),
        scratch_shapes=[pltpu.VMEM((k, tn), BF16)],
        compiler_params=_cp("arbitrary", "arbitrary"),
        name="out_proj",
    )(u, w, xall, gate)


def _gla_prep_kernel(qk_ref, sm_ref, w2_ref, b2_ref, e_ref, pe_ref, cos_ref, sin_ref,
                     q_out, k_out, la_out):
    z = jnp.dot(sm_ref[...], w2_ref[...], precision=HI, preferred_element_type=F32) + b2_ref[...]
    la = _log_sigmoid(z) * (1.0 / GLA_TAU)
    la_out[0] = la[:, :GLA_PAD]
    la_out[1] = la[:, GLA_PAD:]
    cos = cos_ref[...]
    sin = sin_ref[...]
    e = e_ref[...]
    pe = pe_ref[...]
    q = qk_ref[:, :A_QK]
    k = qk_ref[:, A_QK:]
    rope = lambda t: (jnp.dot(t, e, preferred_element_type=F32) * cos
                      + jnp.dot(t, pe, preferred_element_type=F32) * sin)
    q_out[...] = rope(q) * (GLA_DK ** -0.5)
    k_out[...] = rope(k)


def _gla_prep(p_a, small, w2pad, b2pad, e_mat, pe_mat, cos_t, sin_t, tr, pos_tile):
    r = p_a.shape[0]
    full = lambda i: (0, 0)
    return pl.pallas_call(
        _gla_prep_kernel,
        grid=(r // tr,),
        in_specs=[pl.BlockSpec((tr, 2 * A_QK), lambda i: (i, 0)),
                  pl.BlockSpec((tr, LANE), lambda i: (i, 0)),
                  pl.BlockSpec(w2pad.shape, full),
                  pl.BlockSpec(b2pad.shape, full),
                  pl.BlockSpec(e_mat.shape, full),
                  pl.BlockSpec(pe_mat.shape, full),
                  pl.BlockSpec((tr, GLA_PAD), lambda i: (pos_tile(i), 0)),
                  pl.BlockSpec((tr, GLA_PAD), lambda i: (pos_tile(i), 0))],
        out_specs=[pl.BlockSpec((tr, GLA_PAD), lambda i: (i, 0)),
                   pl.BlockSpec((tr, GLA_PAD), lambda i: (i, 0)),
                   pl.BlockSpec((2, tr, GLA_PAD), lambda i: (0, i, 0))],
        out_shape=[jax.ShapeDtypeStruct((r, GLA_PAD), F32),
                   jax.ShapeDtypeStruct((r, GLA_PAD), F32),
                   jax.ShapeDtypeStruct((2, r, GLA_PAD), F32)],
        compiler_params=_cp("arbitrary"),
        name="gla_prep",
    )(p_a, small, w2pad, b2pad, e_mat, pe_mat, cos_t, sin_t)


def _gla_consts():
    c = GLA_CHUNK
    mats, masks = [], []
    idx = np.arange(c)
    for d in range(2):
        pos = idx if d == 0 else c - 1 - idx
        rb = pos // GLA_SUB
        rh = pos // (2 * GLA_SUB)
        le = pos[None, :] <= pos[:, None]
        gt = pos[None, :] > pos[:, None]
        same_b = rb[None, :] == rb[:, None]
        same_h = rh[None, :] == rh[:, None]
        m = [le, gt, le & same_b, gt & same_b, le & same_h, gt & same_h, np.ones((8, c), bool)]
        mats.append(np.concatenate([a.astype(np.float32) for a in m], axis=0))
        md = le & same_b
        m16 = same_h & (rb[None, :] < rb[:, None])
        m32 = rh[None, :] < rh[:, None]
        masks.append(np.stack([md, m16, m32]).astype(np.float32))
    return np.stack(mats), np.stack(masks)


def _gla_scan_kernel(q_ref, k_ref, v_ref, la_ref, mats_ref, mask_ref, y_ref, st_ref):
    c = GLA_CHUNK

    @pl.when(pl.program_id(2) == 0)
    def _():
        st_ref[...] = jnp.zeros(st_ref.shape, F32)

    mats = mats_ref[0]
    md = mask_ref[0, 0] > 0.0
    m16 = mask_ref[0, 1]
    m32 = mask_ref[0, 2]
    for h in range(GLA_HEADS):
        sl = slice(h * LANE, (h + 1) * LANE)
        args = jnp.dot(mats, la_ref[0, :, sl], precision=HI, preferred_element_type=F32)
        dq = args[2 * c:3 * c]
        q = q_ref[:, sl]
        k = k_ref[:, sl]
        v = v_ref[:, sl]
        qd = (q * jnp.exp(dq)).astype(BF16)
        kd = (k * jnp.exp(-dq)).astype(BF16)
        k16 = (k * jnp.exp(args[3 * c:4 * c])).astype(BF16)
        q32 = (q * jnp.exp(args[4 * c:5 * c])).astype(BF16)
        k32 = (k * jnp.exp(args[5 * c:6 * c])).astype(BF16)
        att = jnp.where(md, _nt(qd, kd), 0.0) + m16 * _nt(qd, k16) + m32 * _nt(q32, k32)
        st = st_ref[h]
        qi = (q * jnp.exp(args[0:c])).astype(BF16)
        y_ref[0, :, sl] = (jnp.dot(att.astype(BF16), v, preferred_element_type=F32)
                           + _nt(qi, st.astype(BF16)))
        ka = (k * jnp.exp(args[c:2 * c])).astype(BF16)
        st_ref[h] = st * jnp.exp(args[6 * c:6 * c + 1]) + _tn(v, ka)


def _chunk_index(bsz, n_lat, n_ctx, lat0, ctx0):
    def f(b, d, c):
        in_ctx = c < n_ctx
        cc = jnp.where(d == 0, c, n_ctx - 1 - c)
        cl = jnp.where(d == 0, c - n_ctx, n_lat - 1 - (c - n_ctx))
        return jnp.where(in_ctx, ctx0 + b * n_ctx + cc, lat0 + b * n_lat + cl)
    return f


def _gla_scan(qr, kr, p_a, la, bsz, seq, ctx_len):
    r = qr.shape[0]
    c = GLA_CHUNK
    mats, masks = _gla_consts()
    n_lat, n_ctx = seq // c, ctx_len // c
    blk = _chunk_index(bsz, n_lat, n_ctx, 0, bsz * n_lat)
    v_col = (2 * A_QK) // A_V
    return pl.pallas_call(
        _gla_scan_kernel,
        grid=(bsz, 2, n_lat + n_ctx),
        in_specs=[pl.BlockSpec((c, GLA_PAD), lambda b, d, i: (blk(b, d, i), 0)),
                  pl.BlockSpec((c, GLA_PAD), lambda b, d, i: (blk(b, d, i), 0)),
                  pl.BlockSpec((c, A_V), lambda b, d, i: (blk(b, d, i), v_col)),
                  pl.BlockSpec((1, c, GLA_PAD), lambda b, d, i: (d, blk(b, d, i), 0)),
                  pl.BlockSpec((1,) + mats.shape[1:], lambda b, d, i: (d, 0, 0)),
                  pl.BlockSpec((1,) + masks.shape[1:], lambda b, d, i: (d, 0, 0, 0))],
        out_specs=pl.BlockSpec((1, c, A_V), lambda b, d, i: (d, blk(b, d, i), 0)),
        out_shape=jax.ShapeDtypeStruct((2, r, A_V), F32),
        scratch_shapes=[pltpu.VMEM((GLA_HEADS, GLA_DV, LANE), F32)],
        compiler_params=_cp("arbitrary", "arbitrary", "arbitrary"),
        name="gla_scan",
    )(qr, kr, p_a, la, jnp.asarray(mats), jnp.asarray(masks))


def _ml_prep_kernel(x_ref, xp_ref, xn_ref, w_ref, b_ref, first_ref, o_ref):
    i = pl.program_id(0)
    x = x_ref[...].astype(F32)
    tr = x.shape[0]
    is_first = first_ref[i, 0] > 0
    is_last = first_ref[i, 1] > 0
    prev = jnp.where(is_first, 0.0, xp_ref[7:8, :].astype(F32))
    nxt = jnp.where(is_last, 0.0, xn_ref[0:1, :].astype(F32))
    row = lax.broadcasted_iota(jnp.int32, x.shape, 0)
    xm1 = jnp.where(row == 0, prev, pltpu.roll(x, 1, axis=0))
    xp1 = jnp.where(row == tr - 1, nxt, pltpu.roll(x, tr - 1, axis=0))
    y = xm1 * w_ref[0:1, :] + x * w_ref[1:2, :] + xp1 * w_ref[2:3, :] + b_ref[...]
    y = _silu(y)
    col = lax.broadcasted_iota(jnp.int32, x.shape, 1)
    o_ref[...] = jnp.where(col >= B_W, y * (ML_DH ** -0.5), y).astype(BF16)


def _ml_prep(p_b, conv_w, conv_b, tr, edge_flags):
    r = p_b.shape[0]
    w = 2 * B_W
    nb = r // 8
    return pl.pallas_call(
        _ml_prep_kernel,
        grid=(r // tr,),
        in_specs=[pl.BlockSpec((tr, w), lambda i: (i, 0)),
                  pl.BlockSpec((8, w), lambda i: (jnp.maximum(i * (tr // 8) - 1, 0), 0)),
                  pl.BlockSpec((8, w), lambda i: (jnp.minimum((i + 1) * (tr // 8), nb - 1), 0)),
                  pl.BlockSpec((8, w), lambda i: (0, 0)),
                  pl.BlockSpec((1, w), lambda i: (0, 0)),
                  pl.BlockSpec(memory_space=pltpu.SMEM)],
        out_specs=pl.BlockSpec((tr, w), lambda i: (i, 0)),
        out_shape=jax.ShapeDtypeStruct((r, w), BF16),
        compiler_params=_cp("arbitrary"),
        name="mlstm_prep",
    )(p_b, p_b, p_b, jnp.pad(conv_w, ((0, 8 - conv_w.shape[0]), (0, 0))), conv_b.reshape(1, w),
      edge_flags)


def _ml_scan_kernel(qk_ref, v_ref, gi_ref, gf_ref, git_ref, gft_ref, tri_ref, y_ref,
                    c_ref, n_ref, m_ref):
    @pl.when(pl.program_id(2) == 0)
    def _():
        c_ref[...] = jnp.zeros(c_ref.shape, F32)
        n_ref[...] = jnp.zeros(n_ref.shape, F32)
        m_ref[...] = jnp.zeros(m_ref.shape, F32)

    tri = tri_ref[0]
    causal = tri > 0.0
    lf = _log_sigmoid(gf_ref[0])
    lft = _log_sigmoid(gft_ref[0])
    bcol = jnp.dot(tri, lf, precision=HI, preferred_element_type=F32)
    brow = lax.dot_general(lft, tri, (((1,), (1,)), ((), ())), precision=HI,
                           preferred_element_type=F32)
    tot = jnp.sum(lf, axis=0, keepdims=True)
    gi = gi_ref[0]
    git = git_ref[0]
    for h in range(ML_HEADS):
        sl = slice(h * ML_DH, (h + 1) * ML_DH)
        q = qk_ref[:, sl]
        k = qk_ref[:, B_W + h * ML_DH:B_W + (h + 1) * ML_DH]
        v = v_ref[:, sl]
        bc = bcol[:, h:h + 1]
        ic = gi[:, h:h + 1]
        br = brow[h:h + 1, :]
        ir = git[h:h + 1, :]
        m_prev = m_ref[h]
        dlog = jnp.where(causal, bc - br + ir, NEG)
        inter = bc + m_prev
        m = jnp.maximum(inter, jnp.max(dlog, axis=1, keepdims=True))
        w_inter = jnp.exp(inter - m)
        qk = _nt(q, k) * jnp.exp(dlog - m)
        cm = c_ref[h]
        nm = n_ref[h]
        num = (jnp.dot(qk.astype(BF16), v, preferred_element_type=F32)
               + w_inter * jnp.dot(q, cm.astype(BF16), preferred_element_type=F32))
        den = (jnp.sum(qk, axis=1, keepdims=True)
               + w_inter * jnp.sum(q.astype(F32) * nm, axis=1, keepdims=True))
        y_ref[0, :, sl] = num / jnp.maximum(jnp.abs(den), jnp.exp(-m))
        t_h = tot[:, h:h + 1]
        log_w = t_h - bc + ic
        m_new = jnp.maximum(t_h + m_prev, jnp.max(log_w, axis=0, keepdims=True))
        decay = jnp.exp(t_h + m_prev - m_new)
        kw = k.astype(F32) * jnp.exp(log_w - m_new)
        c_ref[h] = decay * cm + _tn(kw.astype(BF16), v)
        n_ref[h] = decay * nm + jnp.sum(kw, axis=0, keepdims=True)
        m_ref[h] = m_new


def _ml_scan(qk, p_b, gi, gf, git, gft, bsz, seq, ctx_len):
    r = qk.shape[0]
    c = ML_CHUNK
    idx = np.arange(c)
    tri = np.stack([(idx[None, :] <= idx[:, None]), (idx[None, :] >= idx[:, None])]).astype(np.float32)
    n_lat, n_ctx = seq // c, ctx_len // c
    blk = _chunk_index(bsz, n_lat, n_ctx, 0, bsz * n_lat)
    h = ML_HEADS
    return pl.pallas_call(
        _ml_scan_kernel,
        grid=(bsz, 2, n_lat + n_ctx),
        in_specs=[pl.BlockSpec((c, 2 * B_W), lambda b, d, i: (blk(b, d, i), 0)),
                  pl.BlockSpec((c, B_W), lambda b, d, i: (blk(b, d, i), 2)),
                  pl.BlockSpec((1, c, h), lambda b, d, i: (d, blk(b, d, i), 0)),
                  pl.BlockSpec((1, c, h), lambda b, d, i: (d, blk(b, d, i), 0)),
                  pl.BlockSpec((1, h, c), lambda b, d, i: (d, 0, blk(b, d, i))),
                  pl.BlockSpec((1, h, c), lambda b, d, i: (d, 0, blk(b, d, i))),
                  pl.BlockSpec((1, c, c), lambda b, d, i: (d, 0, 0))],
        out_specs=pl.BlockSpec((1, c, B_W), lambda b, d, i: (d, blk(b, d, i), 0)),
        out_shape=jax.ShapeDtypeStruct((2, r, B_W), F32),
        scratch_shapes=[pltpu.VMEM((h, ML_DH, ML_DH), F32),
                        pltpu.VMEM((h, 1, ML_DH), F32),
                        pltpu.VMEM((h, 1, 1), F32)],
        compiler_params=_cp("arbitrary", "arbitrary", "arbitrary"),
        name="mlstm_scan",
    )(qk, p_b, gi, gf, git, gft, jnp.asarray(tri))


def _post_kernel(ya_ref, za_ref, g_ref, hb_ref, zb_ref, ob_ref, oa_ref, obb_ref):
    for h in range(GLA_HEADS):
        sl = slice(h * GLA_DV, (h + 1) * GLA_DV)
        o = ya_ref[0, :, sl] + ya_ref[1, :, sl]
        o = o * lax.rsqrt(jnp.mean(o * o, axis=1, keepdims=True) + EPS) * g_ref[:, sl]
        oa_ref[:, sl] = (o * _silu(za_ref[:, sl].astype(F32))).astype(BF16)
    hb = hb_ref[0] + hb_ref[1]
    obb_ref[...] = (jax.nn.sigmoid(ob_ref[...].astype(F32)) * hb
                    * _silu(zb_ref[...].astype(F32))).astype(BF16)


def _post(y_a, p_a, gnorm, h_b, p_b, tr):
    r = p_a.shape[0]
    za_col = (2 * A_QK + A_V) // A_V
    return pl.pallas_call(
        _post_kernel,
        grid=(r // tr,),
        in_specs=[pl.BlockSpec((2, tr, A_V), lambda i: (0, i, 0)),
                  pl.BlockSpec((tr, A_V), lambda i: (i, za_col)),
                  pl.BlockSpec((1, A_V), lambda i: (0, 0)),
                  pl.BlockSpec((2, tr, B_W), lambda i: (0, i, 0)),
                  pl.BlockSpec((tr, B_W), lambda i: (i, 3)),
                  pl.BlockSpec((tr, B_W), lambda i: (i, 4))],
        out_specs=[pl.BlockSpec((tr, A_V), lambda i: (i, 0)),
                   pl.BlockSpec((tr, B_W), lambda i: (i, 0))],
        out_shape=[jax.ShapeDtypeStruct((r, A_V), BF16),
                   jax.ShapeDtypeStruct((r, B_W), BF16)],
        compiler_params=_cp("arbitrary"),
        name="post_ab",
    )(y_a, p_a, gnorm.reshape(1, A_V), h_b, p_b, p_b)


def _na_kernel(q_ref, k_ref, v_ref, z_ref, kc_ref, vc_ref, t_ref, o_ref, *, rows):
    kc = kc_ref[...]
    vc = vc_ref[...]
    win = NA_KH * GRID_W

    def body(r, carry):
        rs = jnp.clip(r - NA_KH // 2, 0, rows - NA_KH)
        q0 = pl.multiple_of(r * GRID_W, GRID_W)
        k0 = pl.multiple_of(rs * GRID_W, GRID_W)
        q = (q_ref[pl.ds(q0, GRID_W), :].astype(F32) * (NA_DH ** -0.5)).astype(BF16)
        kw = k_ref[pl.ds(k0, win), :]
        vw = v_ref[pl.ds(k0, win), :]
        s_w = _nt(q, kw) + t_ref[0, r - rs]
        s_c = _nt(q, kc)
        m = jnp.maximum(jnp.max(s_w, axis=1, keepdims=True), jnp.max(s_c, axis=1, keepdims=True))
        p_w = jnp.exp(s_w - m)
        p_c = jnp.exp(s_c - m)
        l = jnp.sum(p_w, axis=1, keepdims=True) + jnp.sum(p_c, axis=1, keepdims=True)
        o = (jnp.dot(p_w.astype(BF16), vw, preferred_element_type=F32)
             + jnp.dot(p_c.astype(BF16), vc, preferred_element_type=F32)) / l
        z = z_ref[pl.ds(q0, GRID_W), :].astype(F32)
        o_ref[pl.ds(q0, GRID_W), :] = (o * _silu(z)).astype(BF16)
        return carry

    lax.fori_loop(0, rows, body, 0)


def _na_bias_table(rpb):
    c = np.arange(GRID_W)
    cs = np.clip(c - NA_KW // 2, 0, GRID_W - NA_KW)
    valid = (c[None, :] >= cs[:, None]) & (c[None, :] < cs[:, None] + NA_KW)
    ci = np.clip(c[None, :] - c[:, None] + NA_KW - 1, 0, 2 * NA_KW - 2)
    a = np.arange(NA_KH)
    ri = np.clip(a[None, :] - a[:, None] + NA_KH - 1, 0, 2 * NA_KH - 2)
    t = rpb[:, ri[:, :, None, None], ci[None, None, :, :]]
    t = jnp.where(valid[None, None, None], t, NEG)
    t = jnp.transpose(t, (0, 1, 3, 2, 4))
    return t.reshape(NA_HEADS, NA_KH, GRID_W, NA_KH * GRID_W).astype(F32)


def _na(p_c, table, bsz, seq, ctx_len):
    r = p_c.shape[0]
    rows = seq // GRID_W
    ctx_blk0 = (bsz * seq) // ctx_len
    hq, hk, hv, hz = 0, NA_HEADS, 2 * NA_HEADS, 3 * NA_HEADS
    lat = lambda off: pl.BlockSpec((seq, NA_DH), lambda b, h: (b, off + h))
    ctx = lambda off: pl.BlockSpec((ctx_len, NA_DH), lambda b, h: (ctx_blk0 + b, off + h))
    return pl.pallas_call(
        functools.partial(_na_kernel, rows=rows),
        grid=(bsz, NA_HEADS),
        in_specs=[lat(hq), lat(hk), lat(hv), lat(hz), ctx(hk), ctx(hv),
                  pl.BlockSpec((1,) + table.shape[1:], lambda b, h: (h, 0, 0, 0))],
        out_specs=pl.BlockSpec((seq, NA_DH), lambda b, h: (b, h)),
        out_shape=jax.ShapeDtypeStruct((r, C_W), BF16),
        compiler_params=_cp("arbitrary", "arbitrary"),
        name="nbr_attn",
    )(p_c, p_c, p_c, p_c, p_c, p_c, table)


def _ctx_attn_kernel(q_ref, k_ref, v_ref, z_ref, y_in_ref, o_ref):
    del y_in_ref
    q = (q_ref[...].astype(F32) * (NA_DH ** -0.5)).astype(BF16)
    s = _nt(q, k_ref[...])
    m = jnp.max(s, axis=1, keepdims=True)
    p = jnp.exp(s - m)
    o = jnp.dot(p.astype(BF16), v_ref[...], preferred_element_type=F32) / jnp.sum(p, axis=1, keepdims=True)
    o_ref[...] = (o * _silu(z_ref[...].astype(F32))).astype(BF16)


def _ctx_attn(p_c, y_c, bsz, seq, ctx_len):
    ctx_blk0 = (bsz * seq) // ctx_len
    spec = lambda off: pl.BlockSpec((ctx_len, NA_DH), lambda b, h: (ctx_blk0 + b, off + h))
    return pl.pallas_call(
        _ctx_attn_kernel,
        grid=(bsz, NA_HEADS),
        in_specs=[spec(0), spec(NA_HEADS), spec(2 * NA_HEADS), spec(3 * NA_HEADS),
                  pl.BlockSpec(memory_space=pl.ANY)],
        out_specs=spec(0),
        out_shape=jax.ShapeDtypeStruct(y_c.shape, y_c.dtype),
        input_output_aliases={4: 0},
        compiler_params=_cp("arbitrary", "arbitrary"),
        name="ctx_attn",
    )(p_c, p_c, p_c, p_c, y_c)


def _rope_tables(seq, ctx_len):
    nf = GLA_DK // 4
    t = np.arange(seq)
    inv = ROPE_BASE ** (-np.arange(nf, dtype=np.float32) / nf)
    ar = (t // GRID_W).astype(np.float32)[:, None] * inv[None, :]
    ac = (t % GRID_W).astype(np.float32)[:, None] * inv[None, :]
    ar, ac = jnp.asarray(ar), jnp.asarray(ac)
    cos = jnp.concatenate([jnp.cos(ar), jnp.cos(ar), jnp.cos(ac), jnp.cos(ac)], axis=1)
    sin = jnp.concatenate([-jnp.sin(ar), jnp.sin(ar), -jnp.sin(ac), jnp.sin(ac)], axis=1)
    cos = jnp.concatenate([cos, jnp.ones((ctx_len, GLA_DK), F32)], axis=0)
    sin = jnp.concatenate([sin, jnp.zeros((ctx_len, GLA_DK), F32)], axis=0)
    pad = jnp.zeros((seq + ctx_len, LANE - GLA_DK), F32)
    cos = jnp.tile(jnp.concatenate([cos, pad], axis=1), (1, GLA_HEADS))
    sin = jnp.tile(jnp.concatenate([sin, pad], axis=1), (1, GLA_HEADS))
    return cos, sin


def _rope_mats():
    e = np.zeros((A_QK, GLA_PAD), np.float32)
    pe = np.zeros((A_QK, GLA_PAD), np.float32)
    quarter = GLA_DK // 4
    for i in range(A_QK):
        h, j = divmod(i, GLA_DK)
        e[i, h * LANE + j] = 1.0
    for h in range(GLA_HEADS):
        for j in range(GLA_DK):
            src = j + quarter if (j % (2 * quarter)) < quarter else j - quarter
            pe[h * GLA_DK + src, h * LANE + j] = 1.0
    return jnp.asarray(e, BF16), jnp.asarray(pe, BF16)


def _alpha_weights(w_alpha2, b_alpha):
    w = jnp.zeros((LANE, 2, GLA_HEADS, LANE), F32)
    b = jnp.zeros((2, GLA_HEADS, LANE), F32)
    for r in range(2):
        wr = w_alpha2[r].reshape(GLA_RANK, GLA_HEADS, GLA_DK)
        w = w.at[r * GLA_RANK:(r + 1) * GLA_RANK, r, :, :GLA_DK].set(wr)
        b = b.at[r, :, :GLA_DK].set(b_alpha[r].reshape(GLA_HEADS, GLA_DK))
    return w.reshape(LANE, 2 * GLA_PAD), b.reshape(1, 2 * GLA_PAD)


def kernel(x, c, ctx, c_ctx, w_mod, b_mod, norm_g, w_in, w_alpha2, b_alpha, gla_norm_g, b_gates,
           conv_w, conv_b, rpb, w_merge, b_merge, w_proj_a, w_proj_b, w_proj_c, w_out, final_g):
    bsz, seq, d = x.shape
    ctx_len = ctx.shape[1]
    depth = w_mod.shape[0]
    n_lat_rows = bsz * seq
    r_all = n_lat_rows + bsz * ctx_len
    tr = min(256, ctx_len)
    assert bsz < 8 and seq % tr == 0 and ctx_len % tr == 0 and tr % ML_CHUNK == 0
    assert seq % GRID_W == 0 and seq // GRID_W >= NA_KH and n_lat_rows % ctx_len == 0
    tm = _pick(seq, (512, 256, 128))
    tm_all = tm if r_all % tm == 0 else tr

    lat_tiles = seq // tr
    group_tr = lambda i: jnp.minimum(i // lat_tiles, bsz)
    group_m = lambda tm_: (lambda i: jnp.minimum(i // (seq // tm_), bsz))
    n_lat_t, ctx_tiles = n_lat_rows // tr, ctx_len // tr
    pos_tile = lambda i: jnp.where(i < n_lat_t, i % lat_tiles, lat_tiles + (i - n_lat_t) % ctx_tiles)
    ti = np.arange(r_all // tr)
    in_lat = ti < n_lat_t
    first = np.where(in_lat, ti % lat_tiles == 0, (ti - n_lat_t) % ctx_tiles == 0)
    last = np.where(in_lat, ti % lat_tiles == lat_tiles - 1, (ti - n_lat_t) % ctx_tiles == ctx_tiles - 1)
    edge_flags = jnp.asarray(np.stack([first, last], axis=1).astype(np.int32))

    cc = jnp.zeros((8, d), F32).at[:bsz].set(c).at[bsz].set(c_ctx)
    mod = _modulation(cc, w_mod, b_mod)
    cos_t, sin_t = _rope_tables(seq, ctx_len)
    e_mat, pe_mat = _rope_mats()

    xall = jnp.concatenate([x.reshape(n_lat_rows, d), ctx.reshape(bsz * ctx_len, d)], axis=0)
    for l in range(depth):
        last_layer = l == depth - 1
        shift = mod[l, :, :d].reshape(8, 1, d)
        scale = mod[l, :, d:2 * d].reshape(8, 1, d)
        gate = mod[l, :, 2 * d:].reshape(8, 1, d)
        wl = w_in[l]
        w_small = jnp.zeros((d, LANE), F32)
        w_small = w_small.at[:, :2 * GLA_RANK].set(wl[:, OFF_AA:OFF_B])
        w_small = w_small.at[:, 2 * GLA_RANK:2 * GLA_RANK + 4 * ML_HEADS].set(wl[:, OFF_GATES:OFF_C])
        h, small = _norm_mod(xall, norm_g[l], shift, scale, w_small, tr, group_tr)

        p_a = _mm(h, wl[:, :OFF_AA], r_all, tm_all, name="in_proj_a")
        p_b = _mm(h, wl[:, OFF_B:OFF_GATES], r_all, tm_all, name="in_proj_b")
        p_c = _mm(h, wl[:, OFF_C:], r_all, tm_all, name="in_proj_c")

        w2pad, b2pad = _alpha_weights(w_alpha2[l], b_alpha[l])
        qr, kr, la = _gla_prep(p_a, small, w2pad, b2pad, e_mat, pe_mat, cos_t, sin_t, tr, pos_tile)
        y_a = _gla_scan(qr, kr, p_a, la, bsz, seq, ctx_len)

        qk_b = _ml_prep(p_b, conv_w[l], conv_b[l], tr, edge_flags)
        g = small[:, 2 * GLA_RANK:2 * GLA_RANK + 4 * ML_HEADS].reshape(r_all, 4, ML_HEADS) + b_gates[l]
        gi = jnp.stack([g[:, 0], g[:, 2]])
        gf = jnp.stack([g[:, 1], g[:, 3]])
        h_b = _ml_scan(qk_b, p_b, gi, gf, jnp.swapaxes(gi, 1, 2), jnp.swapaxes(gf, 1, 2),
                       bsz, seq, ctx_len)
        ya, yb = _post(y_a, p_a, gla_norm_g[l], h_b, p_b, tr)

        yc = _na(p_c, _na_bias_table(rpb[l]), bsz, seq, ctx_len)
        if not last_layer:
            yc = _ctx_attn(p_c, yc, bsz, seq, ctx_len)

        n_rows = n_lat_rows if last_layer else r_all
        tm_l = tm if last_layer else tm_all
        gates = _mm(h, w_merge[l], n_rows, tm_l, bias=b_merge[l], act="sigmoid_bias", name="merge_gates")
        u = _proj_merge(ya, yb, yc, w_proj_a[l], w_proj_b[l], w_proj_c[l], gates, n_rows, tm_l)
        xall = _out_proj(u, w_out[l], xall, gate, n_rows, tm_l, group_m(tm_l))

    out = _final_norm(xall, final_g, n_lat_rows, tr)
    return out.reshape(bsz, seq, d)
```

```python
import functools

import numpy as np
import jax
import jax.numpy as jnp
from jax import lax
from jax.experimental import pallas as pl
from jax.experimental.pallas import tpu as pltpu

F32 = jnp.float32
BF16 = jnp.bfloat16
HI = lax.Precision.HIGHEST

GRID_W = 64
EPS = 1e-6
ROPE_BASE = 10000.0
GLA_HEADS, GLA_DK, GLA_DV, GLA_RANK, GLA_TAU = 8, 64, 128, 16, 16.0
ML_HEADS, ML_DH = 8, 128
NA_HEADS, NA_DH, NA_KH, NA_KW = 16, 128, 8, 16
A_QK = GLA_HEADS * GLA_DK
A_V = GLA_HEADS * GLA_DV
B_W = ML_HEADS * ML_DH
C_W = NA_HEADS * NA_DH
LANE = 128
GLA_PAD = GLA_HEADS * LANE
GLA_CHUNK = 64
GLA_SUB = 16
ML_CHUNK = 128
NA_UNROLL = 4
NEG = -1e30
OFF_AA = 2 * A_QK + 2 * A_V
OFF_B = OFF_AA + 2 * GLA_RANK
OFF_GATES = OFF_B + 5 * B_W
OFF_C = OFF_GATES + 4 * ML_HEADS
VMEM_LIMIT = 52 * 1024 * 1024


def _cp(*sem):
    return pltpu.CompilerParams(dimension_semantics=sem, vmem_limit_bytes=VMEM_LIMIT)


def _nt(a, b):
    return lax.dot_general(a, b, (((1,), (1,)), ((), ())), preferred_element_type=F32)


def _tn(a, b):
    return lax.dot_general(a, b, (((0,), (0,)), ((), ())), preferred_element_type=F32)


def _log_sigmoid(x):
    return jnp.minimum(x, 0.0) - jnp.log1p(jnp.exp(-jnp.abs(x)))


def _silu(x):
    return x * jax.nn.sigmoid(x)


def _pick(n, cands):
    for c in cands:
        if n % c == 0:
            return c
    raise ValueError(f"no tile for {n}")


def _mod_kernel(s_ref, w_ref, b_ref, o_ref, *, kc):
    s = _silu(s_ref[...]).astype(BF16)
    d = s.shape[1]
    acc = jnp.zeros(o_ref.shape[1:], F32)
    for k0 in range(0, d, kc):
        acc = acc + jnp.dot(s[:, k0:k0 + kc], w_ref[0, k0:k0 + kc, :].astype(BF16),
                            preferred_element_type=F32)
    o_ref[0] = acc + b_ref[0]


def _modulation(cc, w_mod, b_mod):
    depth, d, n = w_mod.shape
    tn = _pick(n, (512, 256, 128))
    kc = _pick(d, (512, 256, 128))
    return pl.pallas_call(
        functools.partial(_mod_kernel, kc=kc),
        grid=(depth, n // tn),
        in_specs=[pl.BlockSpec((8, d), lambda l, j: (0, 0)),
                  pl.BlockSpec((1, d, tn), lambda l, j: (l, 0, j)),
                  pl.BlockSpec((1, 1, tn), lambda l, j: (l, 0, j))],
        out_specs=pl.BlockSpec((1, 8, tn), lambda l, j: (l, 0, j)),
        out_shape=jax.ShapeDtypeStruct((depth, 8, n), F32),
        compiler_params=_cp("arbitrary", "arbitrary"),
        name="modulation",
    )(cc, w_mod, b_mod.reshape(depth, 1, n))


def _norm_kernel(x_ref, g_ref, sh_ref, sc_ref, ws_ref, h_ref, sm_ref):
    x = x_ref[...]
    xn = x * lax.rsqrt(jnp.mean(x * x, axis=1, keepdims=True) + EPS)
    h = xn * g_ref[...] * (1.0 + sc_ref[0]) + sh_ref[0]
    h_ref[...] = h.astype(BF16)
    sm_ref[...] = jnp.dot(h, ws_ref[...], precision=HI, preferred_element_type=F32)


def _norm_mod(xall, g, shift, scale, w_small, tr, group_of):
    r, d = xall.shape
    return pl.pallas_call(
        _norm_kernel,
        grid=(r // tr,),
        in_specs=[pl.BlockSpec((tr, d), lambda i: (i, 0)),
                  pl.BlockSpec((1, d), lambda i: (0, 0)),
                  pl.BlockSpec((1, 1, d), lambda i: (group_of(i), 0, 0)),
                  pl.BlockSpec((1, 1, d), lambda i: (group_of(i), 0, 0)),
                  pl.BlockSpec((d, LANE), lambda i: (0, 0))],
        out_specs=[pl.BlockSpec((tr, d), lambda i: (i, 0)),
                   pl.BlockSpec((tr, LANE), lambda i: (i, 0))],
        out_shape=[jax.ShapeDtypeStruct((r, d), BF16),
                   jax.ShapeDtypeStruct((r, LANE), F32)],
        compiler_params=_cp("arbitrary"),
        name="norm_mod",
    )(xall, g.reshape(1, d), shift, scale, w_small)


def _final_norm_kernel(x_ref, g_ref, o_ref):
    x = x_ref[...]
    o_ref[...] = x * lax.rsqrt(jnp.mean(x * x, axis=1, keepdims=True) + EPS) * g_ref[...]


def _final_norm(x, g, n_rows, tr):
    d = x.shape[1]
    return pl.pallas_call(
        _final_norm_kernel,
        grid=(n_rows // tr,),
        in_specs=[pl.BlockSpec((tr, d), lambda i: (i, 0)),
                  pl.BlockSpec((1, d), lambda i: (0, 0))],
        out_specs=pl.BlockSpec((tr, d), lambda i: (i, 0)),
        out_shape=jax.ShapeDtypeStruct((n_rows, d), F32),
        compiler_params=_cp("arbitrary"),
        name="final_norm",
    )(x, g.reshape(1, d))


def _cast_weight(w_ref, wb_ref, extra_ref=None, shift=0):
    k, tn = wb_ref.shape
    kc = _pick(k, (512, 256, 128))
    for k0 in range(0, k, kc):
        w = w_ref[0, k0:k0 + kc, :]
        if shift:
            w = jnp.concatenate([w, extra_ref[0, k0:k0 + kc, :]], axis=1)[:, shift:shift + tn]
        wb_ref[k0:k0 + kc, :] = w.astype(BF16)


def _weight_specs(w, layer, col0, tn):
    k = w.shape[1]
    base = (col0 // LANE) * LANE
    shift = col0 - base
    assert base % tn == 0
    specs = [pl.BlockSpec((1, k, tn), lambda j, i: (layer, 0, base // tn + j))]
    if shift:
        specs.append(pl.BlockSpec((1, k, LANE), lambda j, i: (layer, 0, (base + (j + 1) * tn) // LANE)))
    return specs, shift


def _mm_kernel(*refs, act, shift):
    x_ref, w_ref = refs[0], refs[1]
    extra_ref = refs[2] if shift else None
    b_ref, o_ref, wb_ref = refs[-3], refs[-2], refs[-1]

    @pl.when(pl.program_id(1) == 0)
    def _():
        _cast_weight(w_ref, wb_ref, extra_ref, shift)

    acc = jnp.dot(x_ref[...], wb_ref[...], preferred_element_type=F32)
    if act == "sigmoid_bias":
        acc = jax.nn.sigmoid(acc + b_ref[0])
    o_ref[...] = acc.astype(o_ref.dtype)


def _mm(x, w, layer, col0, n, n_rows, tm, bias=None, act="none", name="mm"):
    k = w.shape[1]
    tn = _pick(n, (512, 256, 128))
    wspecs, shift = _weight_specs(w, layer, col0, tn)
    if bias is None:
        bias = jnp.zeros((1, 1, n), F32)
        bspec = pl.BlockSpec((1, 1, tn), lambda j, i: (0, 0, j))
    else:
        bspec = pl.BlockSpec((1, 1, tn), lambda j, i: (layer, 0, j))
    return pl.pallas_call(
        functools.partial(_mm_kernel, act=act, shift=shift),
        grid=(n // tn, n_rows // tm),
        in_specs=[pl.BlockSpec((tm, k), lambda j, i: (i, 0))] + wspecs + [bspec],
        out_specs=pl.BlockSpec((tm, tn), lambda j, i: (i, j)),
        out_shape=jax.ShapeDtypeStruct((n_rows, n), BF16),
        scratch_shapes=[pltpu.VMEM((k, tn), BF16)],
        compiler_params=_cp("arbitrary", "arbitrary"),
        name=name,
    )(x, *([w] * len(wspecs)), bias)


def _proj_kernel(ya_ref, yb_ref, yc_ref, wa_ref, wb_ref, wc_ref, ga_ref, gb_ref, gc_ref,
                 o_ref, wa_s, wb_s, wc_s):
    @pl.when(pl.program_id(1) == 0)
    def _():
        _cast_weight(wa_ref, wa_s)
        _cast_weight(wb_ref, wb_s)
        _cast_weight(wc_ref, wc_s)

    u = ga_ref[...].astype(F32) * jnp.dot(ya_ref[...], wa_s[...], preferred_element_type=F32)
    u = u + gb_ref[...].astype(F32) * jnp.dot(yb_ref[...], wb_s[...], preferred_element_type=F32)
    u = u + gc_ref[...].astype(F32) * jnp.dot(yc_ref[...], wc_s[...], preferred_element_type=F32)
    o_ref[...] = u.astype(o_ref.dtype)


def _proj_merge(ya, yb, yc, wa, wb, wc, layer, gates, n_rows, tm):
    d = wa.shape[2]
    tn = _pick(d, (512, 256, 128))
    nj = d // tn
    row = lambda j, i: (i, 0)
    col = lambda j, i: (layer, 0, j)
    return pl.pallas_call(
        _proj_kernel,
        grid=(nj, n_rows // tm),
        in_specs=[pl.BlockSpec((tm, ya.shape[1]), row),
                  pl.BlockSpec((tm, yb.shape[1]), row),
                  pl.BlockSpec((tm, yc.shape[1]), row),
                  pl.BlockSpec((1, wa.shape[1], tn), col),
                  pl.BlockSpec((1, wb.shape[1], tn), col),
                  pl.BlockSpec((1, wc.shape[1], tn), col),
                  pl.BlockSpec((tm, tn), lambda j, i: (i, j)),
                  pl.BlockSpec((tm, tn), lambda j, i: (i, nj + j)),
                  pl.BlockSpec((tm, tn), lambda j, i: (i, 2 * nj + j))],
        out_specs=pl.BlockSpec((tm, tn), lambda j, i: (i, j)),
        out_shape=jax.ShapeDtypeStruct((n_rows, d), BF16),
        scratch_shapes=[pltpu.VMEM((wa.shape[1], tn), BF16),
                        pltpu.VMEM((wb.shape[1], tn), BF16),
                        pltpu.VMEM((wc.shape[1], tn), BF16)],
        compiler_params=_cp("arbitrary", "arbitrary"),
        name="proj_merge",
    )(ya, yb, yc, wa, wb, wc, gates, gates, gates)


def _out_kernel(u_ref, w_ref, x_ref, g_ref, o_ref, wb_ref):
    @pl.when(pl.program_id(1) == 0)
    def _():
        _cast_weight(w_ref, wb_ref)

    acc = jnp.dot(u_ref[...], wb_ref[...], preferred_element_type=F32)
    o_ref[...] = x_ref[...] + g_ref[0] * acc


def _out_proj(u, w, layer, xall, gate, n_rows, tm, group_of_m):
    k, n = w.shape[1:]
    tn = _pick(n, (512, 256, 128))
    return pl.pallas_call(
        _out_kernel,
        grid=(n // tn, n_rows // tm),
        in_specs=[pl.BlockSpec((tm, k), lambda j, i: (i, 0)),
                  pl.BlockSpec((1, k, tn), lambda j, i: (layer, 0, j)),
                  pl.BlockSpec((tm, tn), lambda j, i: (i, j)),
                  pl.BlockSpec((1, 1, tn), lambda j, i: (group_of_m(i), 0, j))],
        out_specs=pl.BlockSpec((tm, tn), lambda j, i: (i, j)),
        out_shape=jax.ShapeDtypeStruct((n_rows, n), F32),
        scratch_shapes=[pltpu.VMEM((k, tn), BF16)],
        compiler_params=_cp("arbitrary", "arbitrary"),
        name="out_proj",
    )(u, w, xall, gate)


def _gla_prep_kernel(qk_ref, sm_ref, w2_ref, b2_ref, e_ref, pe_ref, cos_ref, sin_ref, tri_ref,
                     q_out, k_out, b_out):
    z = jnp.dot(sm_ref[...], w2_ref[...], precision=HI, preferred_element_type=F32) + b2_ref[...]
    la = _log_sigmoid(z) * (1.0 / GLA_TAU)
    for d in range(2):
        rest = la[:, d * GLA_PAD:(d + 1) * GLA_PAD]
        acc = jnp.zeros(rest.shape, F32)
        for _ in range(3):
            part = rest.astype(BF16)
            acc = acc + jnp.dot(tri_ref[d], part, preferred_element_type=F32)
            rest = rest - part.astype(F32)
        b_out[d] = acc
    cos = cos_ref[...]
    sin = sin_ref[...]
    e = e_ref[...]
    pe = pe_ref[...]
    q = qk_ref[:, :A_QK]
    k = qk_ref[:, A_QK:]
    rope = lambda t: (jnp.dot(t, e, preferred_element_type=F32) * cos
                      + jnp.dot(t, pe, preferred_element_type=F32) * sin)
    q_out[...] = rope(q) * (GLA_DK ** -0.5)
    k_out[...] = rope(k)


def _gla_prep(p_a, small, w2pad, b2pad, e_mat, pe_mat, cos_t, sin_t, tr, pos_tile):
    r = p_a.shape[0]
    full = lambda i: (0, 0)
    idx = np.arange(tr)
    same_chunk = (idx[None, :] // GLA_CHUNK) == (idx[:, None] // GLA_CHUNK)
    tri = np.stack([same_chunk & (idx[None, :] <= idx[:, None]),
                    same_chunk & (idx[None, :] >= idx[:, None])]).astype(np.float32)
    tri = jnp.asarray(tri, BF16)
    return pl.pallas_call(
        _gla_prep_kernel,
        grid=(r // tr,),
        in_specs=[pl.BlockSpec((tr, 2 * A_QK), lambda i: (i, 0)),
                  pl.BlockSpec((tr, LANE), lambda i: (i, 0)),
                  pl.BlockSpec(w2pad.shape, full),
                  pl.BlockSpec(b2pad.shape, full),
                  pl.BlockSpec(e_mat.shape, full),
                  pl.BlockSpec(pe_mat.shape, full),
                  pl.BlockSpec((tr, GLA_PAD), lambda i: (pos_tile(i), 0)),
                  pl.BlockSpec((tr, GLA_PAD), lambda i: (pos_tile(i), 0)),
                  pl.BlockSpec(tri.shape, lambda i: (0, 0, 0))],
        out_specs=[pl.BlockSpec((tr, GLA_PAD), lambda i: (i, 0)),
                   pl.BlockSpec((tr, GLA_PAD), lambda i: (i, 0)),
                   pl.BlockSpec((2, tr, GLA_PAD), lambda i: (0, i, 0))],
        out_shape=[jax.ShapeDtypeStruct((r, GLA_PAD), F32),
                   jax.ShapeDtypeStruct((r, GLA_PAD), F32),
                   jax.ShapeDtypeStruct((2, r, GLA_PAD), F32)],
        compiler_params=_cp("arbitrary"),
        name="gla_prep",
    )(p_a, small, w2pad, b2pad, e_mat, pe_mat, cos_t, sin_t, tri)


def _gla_masks(rev):
    c = GLA_CHUNK
    idx = np.arange(c)
    pos = c - 1 - idx if rev else idx
    rb = pos // GLA_SUB
    rh = pos // (2 * GLA_SUB)
    md = (pos[None, :] <= pos[:, None]) & (rb[None, :] == rb[:, None])
    m16 = (rh[None, :] == rh[:, None]) & (rb[None, :] < rb[:, None])
    return np.stack([md, m16]).astype(np.float32)


def _gla_scan_kernel(q_ref, k_ref, v_ref, b_ref, mask_ref, y_ref, st_ref, *, rev):
    sub = GLA_SUB
    nb = GLA_CHUNK // sub
    assert nb == 4

    @pl.when(pl.program_id(1) == 0)
    def _():
        st_ref[...] = jnp.zeros(st_ref.shape, F32)

    md = mask_ref[0] > 0.0
    m16 = mask_ref[1]
    rank = lambda kb: nb - 1 - kb if rev else kb
    by_rank = lambda rk: nb - 1 - rk if rev else rk
    rows = lambda kb: slice(kb * sub, (kb + 1) * sub)
    cat = lambda parts: jnp.concatenate(parts, axis=0)
    zeros = jnp.zeros((sub, LANE), BF16)
    for h in range(GLA_HEADS):
        sl = slice(h * LANE, (h + 1) * LANE)
        b = b_ref[0, :, sl]
        q = q_ref[:, sl]
        k = k_ref[:, sl]
        v = v_ref[:, sl]

        def end(rk):
            kb = by_rank(rk)
            r0 = kb * sub if rev else kb * sub + sub - 1
            return b[r0:r0 + 1]

        tot = end(nb - 1)
        dq = cat([b[rows(kb)] - end(rank(kb) - 1) if rank(kb) > 0 else b[rows(kb)] for kb in range(nb)])
        qd = (q * jnp.exp(dq)).astype(BF16)
        kd = (k * jnp.exp(-dq)).astype(BF16)
        k16 = cat([(k[rows(kb)] * jnp.exp(end(rank(kb)) - b[rows(kb)])).astype(BF16)
                   if rank(kb) % 2 == 0 else zeros for kb in range(nb)])
        half_end = end(nb // 2 - 1)
        q32 = cat([(q[rows(kb)] * jnp.exp(b[rows(kb)] - half_end)).astype(BF16)
                   if rank(kb) >= nb // 2 else zeros for kb in range(nb)])
        k32 = cat([(k[rows(kb)] * jnp.exp(half_end - b[rows(kb)])).astype(BF16)
                   if rank(kb) < nb // 2 else zeros for kb in range(nb)])
        att = jnp.where(md, _nt(qd, kd), 0.0) + m16 * _nt(qd, k16) + _nt(q32, k32)
        st = st_ref[h]
        qi = (q * jnp.exp(b)).astype(BF16)
        y_ref[:, sl] = (jnp.dot(att.astype(BF16), v, preferred_element_type=F32)
                        + _nt(qi, st.astype(BF16)))
        ka = (k * jnp.exp(tot - b)).astype(BF16)
        st_ref[h] = st * jnp.exp(tot) + _tn(v, ka)


def _chunk_index(bsz, n_lat, n_ctx, lat0, ctx0):
    def f(b, d, c):
        in_ctx = c < n_ctx
        cc = jnp.where(d == 0, c, n_ctx - 1 - c)
        cl = jnp.where(d == 0, c - n_ctx, n_lat - 1 - (c - n_ctx))
        return jnp.where(in_ctx, ctx0 + b * n_ctx + cc, lat0 + b * n_lat + cl)
    return f


def _gla_scan(qr, kr, p_a, bcum, bsz, seq, ctx_len, rev):
    r = qr.shape[0]
    c = GLA_CHUNK
    masks = jnp.asarray(_gla_masks(rev))
    n_lat, n_ctx = seq // c, ctx_len // c
    blk = _chunk_index(bsz, n_lat, n_ctx, 0, bsz * n_lat)
    d = int(rev)
    v_col = (2 * A_QK) // A_V
    return pl.pallas_call(
        functools.partial(_gla_scan_kernel, rev=rev),
        grid=(bsz, n_lat + n_ctx),
        in_specs=[pl.BlockSpec((c, GLA_PAD), lambda b, i: (blk(b, d, i), 0)),
                  pl.BlockSpec((c, GLA_PAD), lambda b, i: (blk(b, d, i), 0)),
                  pl.BlockSpec((c, A_V), lambda b, i: (blk(b, d, i), v_col)),
                  pl.BlockSpec((1, c, GLA_PAD), lambda b, i: (d, blk(b, d, i), 0)),
                  pl.BlockSpec(masks.shape, lambda b, i: (0, 0, 0))],
        out_specs=pl.BlockSpec((c, A_V), lambda b, i: (blk(b, d, i), 0)),
        out_shape=jax.ShapeDtypeStruct((r, A_V), F32),
        scratch_shapes=[pltpu.VMEM((GLA_HEADS, GLA_DV, LANE), F32)],
        compiler_params=_cp("arbitrary", "arbitrary"),
        name="gla_scan_bwd" if rev else "gla_scan_fwd",
    )(qr, kr, p_a, bcum, masks)


def _ml_prep_kernel(x_ref, xp_ref, xn_ref, w_ref, b_ref, first_ref, o_ref):
    i = pl.program_id(0)
    x = x_ref[...].astype(F32)
    tr = x.shape[0]
    is_first = first_ref[i, 0] > 0
    is_last = first_ref[i, 1] > 0
    prev = jnp.where(is_first, 0.0, xp_ref[7:8, :].astype(F32))
    nxt = jnp.where(is_last, 0.0, xn_ref[0:1, :].astype(F32))
    row = lax.broadcasted_iota(jnp.int32, x.shape, 0)
    xm1 = jnp.where(row == 0, prev, pltpu.roll(x, 1, axis=0))
    xp1 = jnp.where(row == tr - 1, nxt, pltpu.roll(x, tr - 1, axis=0))
    y = xm1 * w_ref[0:1, :] + x * w_ref[1:2, :] + xp1 * w_ref[2:3, :] + b_ref[...]
    y = _silu(y)
    col = lax.broadcasted_iota(jnp.int32, x.shape, 1)
    o_ref[...] = jnp.where(col >= B_W, y * (ML_DH ** -0.5), y).astype(BF16)


def _ml_prep(p_b, conv_w, conv_b, tr, edge_flags):
    r = p_b.shape[0]
    w = 2 * B_W
    nb = r // 8
    return pl.pallas_call(
        _ml_prep_kernel,
        grid=(r // tr,),
        in_specs=[pl.BlockSpec((tr, w), lambda i: (i, 0)),
                  pl.BlockSpec((8, w), lambda i: (jnp.maximum(i * (tr // 8) - 1, 0), 0)),
                  pl.BlockSpec((8, w), lambda i: (jnp.minimum((i + 1) * (tr // 8), nb - 1), 0)),
                  pl.BlockSpec((8, w), lambda i: (0, 0)),
                  pl.BlockSpec((1, w), lambda i: (0, 0)),
                  pl.BlockSpec(memory_space=pltpu.SMEM)],
        out_specs=pl.BlockSpec((tr, w), lambda i: (i, 0)),
        out_shape=jax.ShapeDtypeStruct((r, w), BF16),
        compiler_params=_cp("arbitrary"),
        name="mlstm_prep",
    )(p_b, p_b, p_b, jnp.pad(conv_w, ((0, 8 - conv_w.shape[0]), (0, 0))), conv_b.reshape(1, w),
      edge_flags)


def _ml_scan_kernel(qk_ref, v_ref, gi_ref, gf_ref, git_ref, gft_ref, tri_ref, y_ref,
                    c_ref, n_ref, m_ref):
    @pl.when(pl.program_id(2) == 0)
    def _():
        c_ref[...] = jnp.zeros(c_ref.shape, F32)
        n_ref[...] = jnp.zeros(n_ref.shape, F32)
        m_ref[...] = jnp.zeros(m_ref.shape, F32)

    tri = tri_ref[0]
    causal = tri > 0.0
    lf = _log_sigmoid(gf_ref[0])
    lft = _log_sigmoid(gft_ref[0])
    bcol = jnp.dot(tri, lf, precision=HI, preferred_element_type=F32)
    brow = lax.dot_general(lft, tri, (((1,), (1,)), ((), ())), precision=HI,
                           preferred_element_type=F32)
    tot = jnp.sum(lf, axis=0, keepdims=True)
    gi = gi_ref[0]
    git = git_ref[0]
    for h in range(ML_HEADS):
        sl = slice(h * ML_DH, (h + 1) * ML_DH)
        q = qk_ref[:, sl]
        k = qk_ref[:, B_W + h * ML_DH:B_W + (h + 1) * ML_DH]
        v = v_ref[:, sl]
        bc = bcol[:, h:h + 1]
        ic = gi[:, h:h + 1]
        br = brow[h:h + 1, :]
        ir = git[h:h + 1, :]
        m_prev = m_ref[h]
        dlog = jnp.where(causal, bc - br + ir, NEG)
        inter = bc + m_prev
        m = jnp.maximum(inter, jnp.max(dlog, axis=1, keepdims=True))
        w_inter = jnp.exp(inter - m)
        qk = _nt(q, k) * jnp.exp(dlog - m)
        cm = c_ref[h]
        nm = n_ref[h]
        num = (jnp.dot(qk.astype(BF16), v, preferred_element_type=F32)
               + w_inter * jnp.dot(q, cm.astype(BF16), preferred_element_type=F32))
        den = (jnp.sum(qk, axis=1, keepdims=True)
               + w_inter * jnp.sum(q.astype(F32) * nm, axis=1, keepdims=True))
        y_ref[0, :, sl] = num / jnp.maximum(jnp.abs(den), jnp.exp(-m))
        t_h = tot[:, h:h + 1]
        log_w = t_h - bc + ic
        m_new = jnp.maximum(t_h + m_prev, jnp.max(log_w, axis=0, keepdims=True))
        decay = jnp.exp(t_h + m_prev - m_new)
        kw = k.astype(F32) * jnp.exp(log_w - m_new)
        c_ref[h] = decay * cm + _tn(kw.astype(BF16), v)
        n_ref[h] = decay * nm + jnp.sum(kw, axis=0, keepdims=True)
        m_ref[h] = m_new


def _ml_scan(qk, p_b, gi, gf, git, gft, bsz, seq, ctx_len):
    r = qk.shape[0]
    c = ML_CHUNK
    idx = np.arange(c)
    tri = np.stack([(idx[None, :] <= idx[:, None]), (idx[None, :] >= idx[:, None])]).astype(np.float32)
    n_lat, n_ctx = seq // c, ctx_len // c
    blk = _chunk_index(bsz, n_lat, n_ctx, 0, bsz * n_lat)
    h = ML_HEADS
    return pl.pallas_call(
        _ml_scan_kernel,
        grid=(bsz, 2, n_lat + n_ctx),
        in_specs=[pl.BlockSpec((c, 2 * B_W), lambda b, d, i: (blk(b, d, i), 0)),
                  pl.BlockSpec((c, B_W), lambda b, d, i: (blk(b, d, i), 2)),
                  pl.BlockSpec((1, c, h), lambda b, d, i: (d, blk(b, d, i), 0)),
                  pl.BlockSpec((1, c, h), lambda b, d, i: (d, blk(b, d, i), 0)),
                  pl.BlockSpec((1, h, c), lambda b, d, i: (d, 0, blk(b, d, i))),
                  pl.BlockSpec((1, h, c), lambda b, d, i: (d, 0, blk(b, d, i))),
                  pl.BlockSpec((1, c, c), lambda b, d, i: (d, 0, 0))],
        out_specs=pl.BlockSpec((1, c, B_W), lambda b, d, i: (d, blk(b, d, i), 0)),
        out_shape=jax.ShapeDtypeStruct((2, r, B_W), F32),
        scratch_shapes=[pltpu.VMEM((h, ML_DH, ML_DH), F32),
                        pltpu.VMEM((h, 1, ML_DH), F32),
                        pltpu.VMEM((h, 1, 1), F32)],
        compiler_params=_cp("arbitrary", "arbitrary", "arbitrary"),
        name="mlstm_scan",
    )(qk, p_b, gi, gf, git, gft, jnp.asarray(tri))


def _post_kernel(yf_ref, yb_ref, za_ref, g_ref, hb_ref, zb_ref, ob_ref, oa_ref, obb_ref):
    for h in range(GLA_HEADS):
        sl = slice(h * GLA_DV, (h + 1) * GLA_DV)
        o = yf_ref[:, sl] + yb_ref[:, sl]
        o = o * lax.rsqrt(jnp.mean(o * o, axis=1, keepdims=True) + EPS) * g_ref[:, sl]
        oa_ref[:, sl] = (o * _silu(za_ref[:, sl].astype(F32))).astype(BF16)
    hb = hb_ref[0] + hb_ref[1]
    obb_ref[...] = (jax.nn.sigmoid(ob_ref[...].astype(F32)) * hb
                    * _silu(zb_ref[...].astype(F32))).astype(BF16)


def _post(y_f, y_b, p_a, gnorm, h_b, p_b, tr):
    r = p_a.shape[0]
    za_col = (2 * A_QK + A_V) // A_V
    return pl.pallas_call(
        _post_kernel,
        grid=(r // tr,),
        in_specs=[pl.BlockSpec((tr, A_V), lambda i: (i, 0)),
                  pl.BlockSpec((tr, A_V), lambda i: (i, 0)),
                  pl.BlockSpec((tr, A_V), lambda i: (i, za_col)),
                  pl.BlockSpec((1, A_V), lambda i: (0, 0)),
                  pl.BlockSpec((2, tr, B_W), lambda i: (0, i, 0)),
                  pl.BlockSpec((tr, B_W), lambda i: (i, 3)),
                  pl.BlockSpec((tr, B_W), lambda i: (i, 4))],
        out_specs=[pl.BlockSpec((tr, A_V), lambda i: (i, 0)),
                   pl.BlockSpec((tr, B_W), lambda i: (i, 0))],
        out_shape=[jax.ShapeDtypeStruct((r, A_V), BF16),
                   jax.ShapeDtypeStruct((r, B_W), BF16)],
        compiler_params=_cp("arbitrary"),
        name="post_ab",
    )(y_f, y_b, p_a, gnorm.reshape(1, A_V), h_b, p_b, p_b)


def _na_kernel(q_ref, k_ref, v_ref, z_ref, kc_ref, vc_ref, t_ref, o_ref, *, rows):
    kc = kc_ref[...]
    vc = vc_ref[...]
    win = NA_KH * GRID_W

    def body(r, carry):
        rs = jnp.clip(r - NA_KH // 2, 0, rows - NA_KH)
        q0 = pl.multiple_of(r * GRID_W, GRID_W)
        k0 = pl.multiple_of(rs * GRID_W, GRID_W)
        q = (q_ref[pl.ds(q0, GRID_W), :].astype(F32) * (NA_DH ** -0.5)).astype(BF16)
        kw = k_ref[pl.ds(k0, win), :]
        vw = v_ref[pl.ds(k0, win), :]
        s_w = _nt(q, kw) + t_ref[0, r - rs]
        s_c = _nt(q, kc)
        m = jnp.maximum(jnp.max(s_w, axis=1, keepdims=True), jnp.max(s_c, axis=1, keepdims=True))
        p_w = jnp.exp(s_w - m)
        p_c = jnp.exp(s_c - m)
        l = jnp.sum(p_w, axis=1, keepdims=True) + jnp.sum(p_c, axis=1, keepdims=True)
        o = (jnp.dot(p_w.astype(BF16), vw, preferred_element_type=F32)
             + jnp.dot(p_c.astype(BF16), vc, preferred_element_type=F32)) / l
        z = z_ref[pl.ds(q0, GRID_W), :].astype(F32)
        o_ref[pl.ds(q0, GRID_W), :] = (o * _silu(z)).astype(BF16)
        return carry

    lax.fori_loop(0, rows, body, 0, unroll=NA_UNROLL)


def _na_bias_table(rpb):
    c = np.arange(GRID_W)
    cs = np.clip(c - NA_KW // 2, 0, GRID_W - NA_KW)
    valid = (c[None, :] >= cs[:, None]) & (c[None, :] < cs[:, None] + NA_KW)
    dc = c[None, :] - c[:, None] + NA_KW - 1
    ndc = 2 * NA_KW - 1
    onehot = ((dc[None] == np.arange(ndc)[:, None, None]) & valid[None]).astype(np.float32)
    full = jnp.einsum("hdj,jck->hdck", rpb, jnp.asarray(onehot), precision=HI)
    full = full + jnp.asarray(np.where(valid, 0.0, NEG).astype(np.float32))
    t = jnp.stack([full[:, NA_KH - 1 - oi:2 * NA_KH - 1 - oi] for oi in range(NA_KH)], axis=1)
    t = jnp.transpose(t, (0, 1, 3, 2, 4))
    return t.reshape(NA_HEADS, NA_KH, GRID_W, NA_KH * GRID_W)


def _na(p_c, table, bsz, seq, ctx_len):
    r = p_c.shape[0]
    rows = seq // GRID_W
    ctx_blk0 = (bsz * seq) // ctx_len
    hq, hk, hv, hz = 0, NA_HEADS, 2 * NA_HEADS, 3 * NA_HEADS
    lat = lambda off: pl.BlockSpec((seq, NA_DH), lambda b, h: (b, off + h))
    ctx = lambda off: pl.BlockSpec((ctx_len, NA_DH), lambda b, h: (ctx_blk0 + b, off + h))
    return pl.pallas_call(
        functools.partial(_na_kernel, rows=rows),
        grid=(bsz, NA_HEADS),
        in_specs=[lat(hq), lat(hk), lat(hv), lat(hz), ctx(hk), ctx(hv),
                  pl.BlockSpec((1,) + table.shape[1:], lambda b, h: (h, 0, 0, 0))],
        out_specs=pl.BlockSpec((seq, NA_DH), lambda b, h: (b, h)),
        out_shape=jax.ShapeDtypeStruct((r, C_W), BF16),
        compiler_params=_cp("arbitrary", "arbitrary"),
        name="nbr_attn",
    )(p_c, p_c, p_c, p_c, p_c, p_c, table)


def _ctx_attn_kernel(q_ref, k_ref, v_ref, z_ref, y_in_ref, o_ref):
    del y_in_ref
    q = (q_ref[...].astype(F32) * (NA_DH ** -0.5)).astype(BF16)
    s = _nt(q, k_ref[...])
    m = jnp.max(s, axis=1, keepdims=True)
    p = jnp.exp(s - m)
    o = jnp.dot(p.astype(BF16), v_ref[...], preferred_element_type=F32) / jnp.sum(p, axis=1, keepdims=True)
    o_ref[...] = (o * _silu(z_ref[...].astype(F32))).astype(BF16)


def _ctx_attn(p_c, y_c, bsz, seq, ctx_len):
    ctx_blk0 = (bsz * seq) // ctx_len
    spec = lambda off: pl.BlockSpec((ctx_len, NA_DH), lambda b, h: (ctx_blk0 + b, off + h))
    return pl.pallas_call(
        _ctx_attn_kernel,
        grid=(bsz, NA_HEADS),
        in_specs=[spec(0), spec(NA_HEADS), spec(2 * NA_HEADS), spec(3 * NA_HEADS),
                  pl.BlockSpec(memory_space=pl.ANY)],
        out_specs=spec(0),
        out_shape=jax.ShapeDtypeStruct(y_c.shape, y_c.dtype),
        input_output_aliases={4: 0},
        compiler_params=_cp("arbitrary", "arbitrary"),
        name="ctx_attn",
    )(p_c, p_c, p_c, p_c, y_c)


def _rope_tables(seq, ctx_len):
    nf = GLA_DK // 4
    t = np.arange(seq)
    inv = ROPE_BASE ** (-np.arange(nf, dtype=np.float32) / nf)
    ar = (t // GRID_W).astype(np.float32)[:, None] * inv[None, :]
    ac = (t % GRID_W).astype(np.float32)[:, None] * inv[None, :]
    ar, ac = jnp.asarray(ar), jnp.asarray(ac)
    cos = jnp.concatenate([jnp.cos(ar), jnp.cos(ar), jnp.cos(ac), jnp.cos(ac)], axis=1)
    sin = jnp.concatenate([-jnp.sin(ar), jnp.sin(ar), -jnp.sin(ac), jnp.sin(ac)], axis=1)
    cos = jnp.concatenate([cos, jnp.ones((ctx_len, GLA_DK), F32)], axis=0)
    sin = jnp.concatenate([sin, jnp.zeros((ctx_len, GLA_DK), F32)], axis=0)
    pad = jnp.zeros((seq + ctx_len, LANE - GLA_DK), F32)
    cos = jnp.tile(jnp.concatenate([cos, pad], axis=1), (1, GLA_HEADS))
    sin = jnp.tile(jnp.concatenate([sin, pad], axis=1), (1, GLA_HEADS))
    return cos, sin


def _rope_mats():
    e = np.zeros((A_QK, GLA_PAD), np.float32)
    pe = np.zeros((A_QK, GLA_PAD), np.float32)
    quarter = GLA_DK // 4
    for i in range(A_QK):
        h, j = divmod(i, GLA_DK)
        e[i, h * LANE + j] = 1.0
    for h in range(GLA_HEADS):
        for j in range(GLA_DK):
            src = j + quarter if (j % (2 * quarter)) < quarter else j - quarter
            pe[h * GLA_DK + src, h * LANE + j] = 1.0
    return jnp.asarray(e, BF16), jnp.asarray(pe, BF16)


def _alpha_weights(w_alpha2, b_alpha):
    w = jnp.zeros((LANE, 2, GLA_HEADS, LANE), F32)
    b = jnp.zeros((2, GLA_HEADS, LANE), F32)
    for r in range(2):
        wr = w_alpha2[r].reshape(GLA_RANK, GLA_HEADS, GLA_DK)
        w = w.at[r * GLA_RANK:(r + 1) * GLA_RANK, r, :, :GLA_DK].set(wr)
        b = b.at[r, :, :GLA_DK].set(b_alpha[r].reshape(GLA_HEADS, GLA_DK))
    return w.reshape(LANE, 2 * GLA_PAD), b.reshape(1, 2 * GLA_PAD)


def kernel(x, c, ctx, c_ctx, w_mod, b_mod, norm_g, w_in, w_alpha2, b_alpha, gla_norm_g, b_gates,
           conv_w, conv_b, rpb, w_merge, b_merge, w_proj_a, w_proj_b, w_proj_c, w_out, final_g):
    bsz, seq, d = x.shape
    ctx_len = ctx.shape[1]
    depth = w_mod.shape[0]
    n_lat_rows = bsz * seq
    r_all = n_lat_rows + bsz * ctx_len
    tr = min(256, ctx_len)
    assert bsz < 8 and seq % tr == 0 and ctx_len % tr == 0 and tr % ML_CHUNK == 0
    assert seq % GRID_W == 0 and seq // GRID_W >= NA_KH and n_lat_rows % ctx_len == 0
    tm = _pick(seq, (512, 256, 128))
    tm_all = tm if r_all % tm == 0 else tr

    lat_tiles = seq // tr
    group_tr = lambda i: jnp.minimum(i // lat_tiles, bsz)
    group_m = lambda tm_: (lambda i: jnp.minimum(i // (seq // tm_), bsz))
    n_lat_t, ctx_tiles = n_lat_rows // tr, ctx_len // tr
    pos_tile = lambda i: jnp.where(i < n_lat_t, i % lat_tiles, lat_tiles + (i - n_lat_t) % ctx_tiles)
    ti = np.arange(r_all // tr)
    in_lat = ti < n_lat_t
    first = np.where(in_lat, ti % lat_tiles == 0, (ti - n_lat_t) % ctx_tiles == 0)
    last = np.where(in_lat, ti % lat_tiles == lat_tiles - 1, (ti - n_lat_t) % ctx_tiles == ctx_tiles - 1)
    edge_flags = jnp.asarray(np.stack([first, last], axis=1).astype(np.int32))

    cc = jnp.zeros((8, d), F32).at[:bsz].set(c).at[bsz].set(c_ctx)
    mod = _modulation(cc, w_mod, b_mod)
    cos_t, sin_t = _rope_tables(seq, ctx_len)
    e_mat, pe_mat = _rope_mats()

    xall = jnp.concatenate([x.reshape(n_lat_rows, d), ctx.reshape(bsz * ctx_len, d)], axis=0)
    for l in range(depth):
        last_layer = l == depth - 1
        shift = mod[l, :, :d].reshape(8, 1, d)
        scale = mod[l, :, d:2 * d].reshape(8, 1, d)
        gate = mod[l, :, 2 * d:].reshape(8, 1, d)
        w_small = jnp.zeros((d, LANE), F32)
        w_small = w_small.at[:, :2 * GLA_RANK].set(w_in[l, :, OFF_AA:OFF_B])
        w_small = w_small.at[:, 2 * GLA_RANK:2 * GLA_RANK + 4 * ML_HEADS].set(w_in[l, :, OFF_GATES:OFF_C])
        h, small = _norm_mod(xall, norm_g[l], shift, scale, w_small, tr, group_tr)

        p_a = _mm(h, w_in, l, 0, OFF_AA, r_all, tm_all, name="in_proj_a")
        p_b = _mm(h, w_in, l, OFF_B, 5 * B_W, r_all, tm_all, name="in_proj_b")
        p_c = _mm(h, w_in, l, OFF_C, 4 * C_W, r_all, tm_all, name="in_proj_c")

        w2pad, b2pad = _alpha_weights(w_alpha2[l], b_alpha[l])
        qr, kr, bcum = _gla_prep(p_a, small, w2pad, b2pad, e_mat, pe_mat, cos_t, sin_t, tr, pos_tile)
        y_f = _gla_scan(qr, kr, p_a, bcum, bsz, seq, ctx_len, rev=False)
        y_b = _gla_scan(qr, kr, p_a, bcum, bsz, seq, ctx_len, rev=True)

        qk_b = _ml_prep(p_b, conv_w[l], conv_b[l], tr, edge_flags)
        g = small[:, 2 * GLA_RANK:2 * GLA_RANK + 4 * ML_HEADS].reshape(r_all, 4, ML_HEADS) + b_gates[l]
        gi = jnp.stack([g[:, 0], g[:, 2]])
        gf = jnp.stack([g[:, 1], g[:, 3]])
        h_b = _ml_scan(qk_b, p_b, gi, gf, jnp.swapaxes(gi, 1, 2), jnp.swapaxes(gf, 1, 2),
                       bsz, seq, ctx_len)
        ya, yb = _post(y_f, y_b, p_a, gla_norm_g[l], h_b, p_b, tr)

        yc = _na(p_c, _na_bias_table(rpb[l]), bsz, seq, ctx_len)
        if not last_layer:
            yc = _ctx_attn(p_c, yc, bsz, seq, ctx_len)

        n_rows = n_lat_rows if last_layer else r_all
        tm_l = tm if last_layer else tm_all
        gates = _mm(h, w_merge, l, 0, w_merge.shape[2], n_rows, tm_l,
                    bias=b_merge.reshape(depth, 1, -1), act="sigmoid_bias", name="merge_gates")
        u = _proj_merge(ya, yb, yc, w_proj_a, w_proj_b, w_proj_c, l, gates, n_rows, tm_l)
        xall = _out_proj(u, w_out, l, xall, gate, n_rows, tm_l, group_m(tm_l))

    out = _final_norm(xall, final_g, n_lat_rows, tr)
    return out.reshape(bsz, seq, d)
```

```python
import functools

import numpy as np
import jax
import jax.numpy as jnp
from jax import lax
from jax.experimental import pallas as pl
from jax.experimental.pallas import tpu as pltpu

F32 = jnp.float32
BF16 = jnp.bfloat16
HI = lax.Precision.HIGHEST

GRID_W = 64
EPS = 1e-6
ROPE_BASE = 10000.0
GLA_HEADS, GLA_DK, GLA_DV, GLA_RANK, GLA_TAU = 8, 64, 128, 16, 16.0
ML_HEADS, ML_DH = 8, 128
NA_HEADS, NA_DH, NA_KH, NA_KW = 16, 128, 8, 16
A_QK = GLA_HEADS * GLA_DK
A_V = GLA_HEADS * GLA_DV
B_W = ML_HEADS * ML_DH
C_W = NA_HEADS * NA_DH
LANE = 128
GLA_PAD = GLA_HEADS * LANE
GLA_CHUNK = 64
GLA_SUB = 16
ML_CHUNK = 128
NA_ROWS = 8
NEG = -1e30
OFF_AA = 2 * A_QK + 2 * A_V
OFF_B = OFF_AA + 2 * GLA_RANK
OFF_GATES = OFF_B + 5 * B_W
OFF_C = OFF_GATES + 4 * ML_HEADS
VMEM_LIMIT = 52 * 1024 * 1024


def _cp(*sem):
    return pltpu.CompilerParams(dimension_semantics=sem, vmem_limit_bytes=VMEM_LIMIT)


def _nt(a, b):
    return lax.dot_general(a, b, (((1,), (1,)), ((), ())), preferred_element_type=F32)


def _tn(a, b):
    return lax.dot_general(a, b, (((0,), (0,)), ((), ())), preferred_element_type=F32)


def _log_sigmoid(x):
    return jnp.minimum(x, 0.0) - jnp.log1p(jnp.exp(-jnp.abs(x)))


def _silu(x):
    return x * jax.nn.sigmoid(x)


def _pick(n, cands):
    for c in cands:
        if n % c == 0:
            return c
    raise ValueError(f"no tile for {n}")


def _mod_kernel(s_ref, w_ref, b_ref, o_ref, *, kc):
    s = _silu(s_ref[...]).astype(BF16)
    d = s.shape[1]
    acc = jnp.zeros(o_ref.shape[1:], F32)
    for k0 in range(0, d, kc):
        acc = acc + jnp.dot(s[:, k0:k0 + kc], w_ref[0, k0:k0 + kc, :].astype(BF16),
                            preferred_element_type=F32)
    o_ref[0] = acc + b_ref[0]


def _modulation(cc, w_mod, b_mod):
    depth, d, n = w_mod.shape
    tn = _pick(n, (512, 256, 128))
    kc = _pick(d, (512, 256, 128))
    return pl.pallas_call(
        functools.partial(_mod_kernel, kc=kc),
        grid=(depth, n // tn),
        in_specs=[pl.BlockSpec((8, d), lambda l, j: (0, 0)),
                  pl.BlockSpec((1, d, tn), lambda l, j: (l, 0, j)),
                  pl.BlockSpec((1, 1, tn), lambda l, j: (l, 0, j))],
        out_specs=pl.BlockSpec((1, 8, tn), lambda l, j: (l, 0, j)),
        out_shape=jax.ShapeDtypeStruct((depth, 8, n), F32),
        compiler_params=_cp("arbitrary", "arbitrary"),
        name="modulation",
    )(cc, w_mod, b_mod.reshape(depth, 1, n))


def _norm_kernel(x_ref, g_ref, sh_ref, sc_ref, ws_ref, h_ref, sm_ref):
    x = x_ref[...]
    xn = x * lax.rsqrt(jnp.mean(x * x, axis=1, keepdims=True) + EPS)
    h = xn * g_ref[...] * (1.0 + sc_ref[0]) + sh_ref[0]
    h_ref[...] = h.astype(BF16)
    sm_ref[...] = jnp.dot(h, ws_ref[...], precision=HI, preferred_element_type=F32)


def _norm_mod(xall, g, shift, scale, w_small, tr, group_of):
    r, d = xall.shape
    return pl.pallas_call(
        _norm_kernel,
        grid=(r // tr,),
        in_specs=[pl.BlockSpec((tr, d), lambda i: (i, 0)),
                  pl.BlockSpec((1, d), lambda i: (0, 0)),
                  pl.BlockSpec((1, 1, d), lambda i: (group_of(i), 0, 0)),
                  pl.BlockSpec((1, 1, d), lambda i: (group_of(i), 0, 0)),
                  pl.BlockSpec((d, LANE), lambda i: (0, 0))],
        out_specs=[pl.BlockSpec((tr, d), lambda i: (i, 0)),
                   pl.BlockSpec((tr, LANE), lambda i: (i, 0))],
        out_shape=[jax.ShapeDtypeStruct((r, d), BF16),
                   jax.ShapeDtypeStruct((r, LANE), F32)],
        compiler_params=_cp("arbitrary"),
        name="norm_mod",
    )(xall, g.reshape(1, d), shift, scale, w_small)


def _final_norm_kernel(x_ref, g_ref, o_ref):
    x = x_ref[...]
    o_ref[...] = x * lax.rsqrt(jnp.mean(x * x, axis=1, keepdims=True) + EPS) * g_ref[...]


def _final_norm(x, g, n_rows, tr):
    d = x.shape[1]
    return pl.pallas_call(
        _final_norm_kernel,
        grid=(n_rows // tr,),
        in_specs=[pl.BlockSpec((tr, d), lambda i: (i, 0)),
                  pl.BlockSpec((1, d), lambda i: (0, 0))],
        out_specs=pl.BlockSpec((tr, d), lambda i: (i, 0)),
        out_shape=jax.ShapeDtypeStruct((n_rows, d), F32),
        compiler_params=_cp("arbitrary"),
        name="final_norm",
    )(x, g.reshape(1, d))


def _row_tile(n_rows):
    for t in range(1152, 127, -16):
        if n_rows % t == 0:
            return t
    raise ValueError(f"no row tile for {n_rows}")


def _cast_weight(w_ref, wb_ref):
    k = wb_ref.shape[0]
    kc = _pick(k, (512, 256, 128))
    for k0 in range(0, k, kc):
        wb_ref[k0:k0 + kc, :] = w_ref[0, k0:k0 + kc, :].astype(BF16)


def _mm_gate_kernel(x_ref, w_ref, b_ref, o_ref, wb_ref):
    @pl.when(pl.program_id(1) == 0)
    def _():
        _cast_weight(w_ref, wb_ref)

    acc = jnp.dot(x_ref[...], wb_ref[...], preferred_element_type=F32)
    o_ref[...] = jax.nn.sigmoid(acc + b_ref[0]).astype(o_ref.dtype)


def _mm_gate(x, w, bias, layer, n_rows, name):
    k, n = w.shape[1:]
    tn = _pick(n, (512, 256, 128))
    tm = _row_tile(n_rows)
    return pl.pallas_call(
        _mm_gate_kernel,
        grid=(n // tn, n_rows // tm),
        in_specs=[pl.BlockSpec((tm, k), lambda j, i: (i, 0)),
                  pl.BlockSpec((1, k, tn), lambda j, i: (layer, 0, j)),
                  pl.BlockSpec((1, 1, tn), lambda j, i: (layer, 0, j))],
        out_specs=pl.BlockSpec((tm, tn), lambda j, i: (i, j)),
        out_shape=jax.ShapeDtypeStruct((n_rows, n), BF16),
        scratch_shapes=[pltpu.VMEM((k, tn), BF16)],
        compiler_params=_cp("arbitrary", "arbitrary"),
        name=name,
    )(x, w, bias)


def _mm_t_kernel(x_ref, w_ref, o_ref, wb_ref):
    @pl.when(pl.program_id(1) == 0)
    def _():
        for r0 in range(0, wb_ref.shape[0], LANE):
            wb_ref[r0:r0 + LANE, :] = w_ref[0, r0:r0 + LANE, :].astype(BF16)

    o_ref[...] = _nt(x_ref[...], wb_ref[...]).astype(o_ref.dtype)


def _mm_t(x, w_t, layer, row0, n, n_rows, name):
    k = w_t.shape[2]
    tn = _pick(n, (512, 256, 128))
    tm = _row_tile(n_rows)
    assert row0 % 8 == 0
    if row0 % tn == 0:
        wspec = pl.BlockSpec((1, tn, k), lambda j, i: (layer, row0 // tn + j, 0))
    else:
        wspec = pl.BlockSpec((pl.Element(1), pl.Element(tn), pl.Element(k)),
                             lambda j, i: (layer, pl.multiple_of(row0 + j * tn, 8), 0))
    return pl.pallas_call(
        _mm_t_kernel,
        grid=(n // tn, n_rows // tm),
        in_specs=[pl.BlockSpec((tm, k), lambda j, i: (i, 0)), wspec],
        out_specs=pl.BlockSpec((tm, tn), lambda j, i: (i, j)),
        out_shape=jax.ShapeDtypeStruct((n_rows, n), BF16),
        scratch_shapes=[pltpu.VMEM((tn, k), BF16)],
        compiler_params=_cp("arbitrary", "arbitrary"),
        name=name,
    )(x, w_t)


def _proj_kernel(ya_ref, yb_ref, yc_ref, wa_ref, wb_ref, wc_ref, ga_ref, gb_ref, gc_ref,
                 o_ref, wa_s, wb_s, wc_s):
    @pl.when(pl.program_id(1) == 0)
    def _():
        _cast_weight(wa_ref, wa_s)
        _cast_weight(wb_ref, wb_s)
        _cast_weight(wc_ref, wc_s)

    u = ga_ref[...].astype(F32) * jnp.dot(ya_ref[...], wa_s[...], preferred_element_type=F32)
    u = u + gb_ref[...].astype(F32) * jnp.dot(yb_ref[...], wb_s[...], preferred_element_type=F32)
    u = u + gc_ref[...].astype(F32) * jnp.dot(yc_ref[...], wc_s[...], preferred_element_type=F32)
    o_ref[...] = u.astype(o_ref.dtype)


def _proj_merge(ya, yb, yc, wa, wb, wc, layer, gates, n_rows):
    d = wa.shape[2]
    tn = _pick(d, (512, 256, 128))
    tm = _row_tile(n_rows)
    nj = d // tn
    row = lambda j, i: (i, 0)
    col = lambda j, i: (layer, 0, j)
    return pl.pallas_call(
        _proj_kernel,
        grid=(nj, n_rows // tm),
        in_specs=[pl.BlockSpec((tm, ya.shape[1]), row),
                  pl.BlockSpec((tm, yb.shape[1]), row),
                  pl.BlockSpec((tm, yc.shape[1]), row),
                  pl.BlockSpec((1, wa.shape[1], tn), col),
                  pl.BlockSpec((1, wb.shape[1], tn), col),
                  pl.BlockSpec((1, wc.shape[1], tn), col),
                  pl.BlockSpec((tm, tn), lambda j, i: (i, j)),
                  pl.BlockSpec((tm, tn), lambda j, i: (i, nj + j)),
                  pl.BlockSpec((tm, tn), lambda j, i: (i, 2 * nj + j))],
        out_specs=pl.BlockSpec((tm, tn), lambda j, i: (i, j)),
        out_shape=jax.ShapeDtypeStruct((n_rows, d), BF16),
        scratch_shapes=[pltpu.VMEM((wa.shape[1], tn), BF16),
                        pltpu.VMEM((wb.shape[1], tn), BF16),
                        pltpu.VMEM((wc.shape[1], tn), BF16)],
        compiler_params=_cp("arbitrary", "arbitrary"),
        name="proj_merge",
    )(ya, yb, yc, wa, wb, wc, gates, gates, gates)


def _out_kernel(u_ref, w_ref, x_ref, g_ref, o_ref, wb_ref, *, bsz, seq):
    @pl.when(pl.program_id(1) == 0)
    def _():
        _cast_weight(w_ref, wb_ref)

    acc = jnp.dot(u_ref[...], wb_ref[...], preferred_element_type=F32)
    tm = acc.shape[0]
    row = pl.program_id(1) * tm + lax.broadcasted_iota(jnp.int32, (tm, 1), 0)
    g = g_ref[bsz:bsz + 1, :]
    for b in reversed(range(bsz)):
        g = jnp.where(row < (b + 1) * seq, g_ref[b:b + 1, :], g)
    o_ref[...] = x_ref[...] + g * acc


def _out_proj(u, w, layer, xall, gate, n_rows, bsz, seq):
    k, n = w.shape[1:]
    tn = _pick(n, (512, 256, 128))
    tm = _row_tile(n_rows)
    return pl.pallas_call(
        functools.partial(_out_kernel, bsz=bsz, seq=seq),
        grid=(n // tn, n_rows // tm),
        in_specs=[pl.BlockSpec((tm, k), lambda j, i: (i, 0)),
                  pl.BlockSpec((1, k, tn), lambda j, i: (layer, 0, j)),
                  pl.BlockSpec((tm, tn), lambda j, i: (i, j)),
                  pl.BlockSpec((8, tn), lambda j, i: (0, j))],
        out_specs=pl.BlockSpec((tm, tn), lambda j, i: (i, j)),
        out_shape=jax.ShapeDtypeStruct((n_rows, n), F32),
        scratch_shapes=[pltpu.VMEM((k, tn), BF16)],
        compiler_params=_cp("arbitrary", "arbitrary"),
        name="out_proj",
    )(u, w, xall, gate)


def _gla_prep_kernel(qk_ref, sm_ref, w2_ref, b2_ref, e_ref, pe_ref, cos_ref, sin_ref, tri_ref,
                     q_out, k_out, b_out):
    z = jnp.dot(sm_ref[...], w2_ref[...], precision=HI, preferred_element_type=F32) + b2_ref[...]
    la = _log_sigmoid(z) * (1.0 / GLA_TAU)
    for d in range(2):
        rest = la[:, d * GLA_PAD:(d + 1) * GLA_PAD]
        acc = jnp.zeros(rest.shape, F32)
        for _ in range(3):
            part = rest.astype(BF16)
            acc = acc + jnp.dot(tri_ref[d], part, preferred_element_type=F32)
            rest = rest - part.astype(F32)
        b_out[d] = acc
    cos = cos_ref[...]
    sin = sin_ref[...]
    e = e_ref[...]
    pe = pe_ref[...]
    q = qk_ref[:, :A_QK]
    k = qk_ref[:, A_QK:]
    rope = lambda t: (jnp.dot(t, e, preferred_element_type=F32) * cos
                      + jnp.dot(t, pe, preferred_element_type=F32) * sin)
    q_out[...] = rope(q) * (GLA_DK ** -0.5)
    k_out[...] = rope(k)


def _gla_prep(p_a, small, w2pad, b2pad, e_mat, pe_mat, cos_t, sin_t, tr, pos_tile):
    r = p_a.shape[0]
    full = lambda i: (0, 0)
    idx = np.arange(tr)
    same_chunk = (idx[None, :] // GLA_CHUNK) == (idx[:, None] // GLA_CHUNK)
    tri = np.stack([same_chunk & (idx[None, :] <= idx[:, None]),
                    same_chunk & (idx[None, :] >= idx[:, None])]).astype(np.float32)
    tri = jnp.asarray(tri, BF16)
    return pl.pallas_call(
        _gla_prep_kernel,
        grid=(r // tr,),
        in_specs=[pl.BlockSpec((tr, 2 * A_QK), lambda i: (i, 0)),
                  pl.BlockSpec((tr, LANE), lambda i: (i, 0)),
                  pl.BlockSpec(w2pad.shape, full),
                  pl.BlockSpec(b2pad.shape, full),
                  pl.BlockSpec(e_mat.shape, full),
                  pl.BlockSpec(pe_mat.shape, full),
                  pl.BlockSpec((tr, GLA_PAD), lambda i: (pos_tile(i), 0)),
                  pl.BlockSpec((tr, GLA_PAD), lambda i: (pos_tile(i), 0)),
                  pl.BlockSpec(tri.shape, lambda i: (0, 0, 0))],
        out_specs=[pl.BlockSpec((tr, GLA_PAD), lambda i: (i, 0)),
                   pl.BlockSpec((tr, GLA_PAD), lambda i: (i, 0)),
                   pl.BlockSpec((2, tr, GLA_PAD), lambda i: (0, i, 0))],
        out_shape=[jax.ShapeDtypeStruct((r, GLA_PAD), F32),
                   jax.ShapeDtypeStruct((r, GLA_PAD), F32),
                   jax.ShapeDtypeStruct((2, r, GLA_PAD), F32)],
        compiler_params=_cp("arbitrary"),
        name="gla_prep",
    )(p_a, small, w2pad, b2pad, e_mat, pe_mat, cos_t, sin_t, tri)


def _gla_masks(rev):
    c = GLA_CHUNK
    idx = np.arange(c)
    pos = c - 1 - idx if rev else idx
    rb = pos // GLA_SUB
    rh = pos // (2 * GLA_SUB)
    md = (pos[None, :] <= pos[:, None]) & (rb[None, :] == rb[:, None])
    m16 = (rh[None, :] == rh[:, None]) & (rb[None, :] < rb[:, None])
    return np.stack([md, m16]).astype(np.float32)


def _gla_scan_kernel(q_ref, k_ref, v_ref, b_ref, mask_ref, y_ref, st_ref, *, rev):
    sub = GLA_SUB
    nb = GLA_CHUNK // sub
    assert nb == 4

    @pl.when(pl.program_id(1) == 0)
    def _():
        st_ref[...] = jnp.zeros(st_ref.shape, F32)

    md = mask_ref[0] > 0.0
    m16 = mask_ref[1]
    rank = lambda kb: nb - 1 - kb if rev else kb
    by_rank = lambda rk: nb - 1 - rk if rev else rk
    rows = lambda kb: slice(kb * sub, (kb + 1) * sub)
    cat = lambda parts: jnp.concatenate(parts, axis=0)
    zeros = jnp.zeros((sub, LANE), BF16)
    for h in range(GLA_HEADS):
        sl = slice(h * LANE, (h + 1) * LANE)
        b = b_ref[0, :, sl]
        q = q_ref[:, sl]
        k = k_ref[:, sl]
        v = v_ref[:, sl]

        def end(rk):
            kb = by_rank(rk)
            r0 = kb * sub if rev else kb * sub + sub - 1
            return b[r0:r0 + 1]

        tot = end(nb - 1)
        dq = cat([b[rows(kb)] - end(rank(kb) - 1) if rank(kb) > 0 else b[rows(kb)] for kb in range(nb)])
        qd = (q * jnp.exp(dq)).astype(BF16)
        kd = (k * jnp.exp(-dq)).astype(BF16)
        k16 = cat([(k[rows(kb)] * jnp.exp(end(rank(kb)) - b[rows(kb)])).astype(BF16)
                   if rank(kb) % 2 == 0 else zeros for kb in range(nb)])
        half_end = end(nb // 2 - 1)
        q32 = cat([(q[rows(kb)] * jnp.exp(b[rows(kb)] - half_end)).astype(BF16)
                   if rank(kb) >= nb // 2 else zeros for kb in range(nb)])
        k32 = cat([(k[rows(kb)] * jnp.exp(half_end - b[rows(kb)])).astype(BF16)
                   if rank(kb) < nb // 2 else zeros for kb in range(nb)])
        att = jnp.where(md, _nt(qd, kd), 0.0) + m16 * _nt(qd, k16) + _nt(q32, k32)
        st = st_ref[h]
        qi = (q * jnp.exp(b)).astype(BF16)
        y_ref[:, sl] = (jnp.dot(att.astype(BF16), v, preferred_element_type=F32)
                        + _nt(qi, st.astype(BF16)))
        ka = (k * jnp.exp(tot - b)).astype(BF16)
        st_ref[h] = st * jnp.exp(tot) + _tn(v, ka)


def _chunk_index(bsz, n_lat, n_ctx, lat0, ctx0):
    def f(b, d, c):
        in_ctx = c < n_ctx
        cc = jnp.where(d == 0, c, n_ctx - 1 - c)
        cl = jnp.where(d == 0, c - n_ctx, n_lat - 1 - (c - n_ctx))
        return jnp.where(in_ctx, ctx0 + b * n_ctx + cc, lat0 + b * n_lat + cl)
    return f


def _gla_scan(qr, kr, p_a, bcum, bsz, seq, ctx_len, rev):
    r = qr.shape[0]
    c = GLA_CHUNK
    masks = jnp.asarray(_gla_masks(rev))
    n_lat, n_ctx = seq // c, ctx_len // c
    blk = _chunk_index(bsz, n_lat, n_ctx, 0, bsz * n_lat)
    d = int(rev)
    v_col = (2 * A_QK) // A_V
    return pl.pallas_call(
        functools.partial(_gla_scan_kernel, rev=rev),
        grid=(bsz, n_lat + n_ctx),
        in_specs=[pl.BlockSpec((c, GLA_PAD), lambda b, i: (blk(b, d, i), 0)),
                  pl.BlockSpec((c, GLA_PAD), lambda b, i: (blk(b, d, i), 0)),
                  pl.BlockSpec((c, A_V), lambda b, i: (blk(b, d, i), v_col)),
                  pl.BlockSpec((1, c, GLA_PAD), lambda b, i: (d, blk(b, d, i), 0)),
                  pl.BlockSpec(masks.shape, lambda b, i: (0, 0, 0))],
        out_specs=pl.BlockSpec((c, A_V), lambda b, i: (blk(b, d, i), 0)),
        out_shape=jax.ShapeDtypeStruct((r, A_V), F32),
        scratch_shapes=[pltpu.VMEM((GLA_HEADS, GLA_DV, LANE), F32)],
        compiler_params=_cp("arbitrary", "arbitrary"),
        name="gla_scan_bwd" if rev else "gla_scan_fwd",
    )(qr, kr, p_a, bcum, masks)


def _ml_prep_kernel(x_ref, xp_ref, xn_ref, w_ref, b_ref, first_ref, o_ref):
    i = pl.program_id(0)
    x = x_ref[...].astype(F32)
    tr = x.shape[0]
    is_first = first_ref[i, 0] > 0
    is_last = first_ref[i, 1] > 0
    prev = jnp.where(is_first, 0.0, xp_ref[7:8, :].astype(F32))
    nxt = jnp.where(is_last, 0.0, xn_ref[0:1, :].astype(F32))
    row = lax.broadcasted_iota(jnp.int32, x.shape, 0)
    xm1 = jnp.where(row == 0, prev, pltpu.roll(x, 1, axis=0))
    xp1 = jnp.where(row == tr - 1, nxt, pltpu.roll(x, tr - 1, axis=0))
    y = xm1 * w_ref[0:1, :] + x * w_ref[1:2, :] + xp1 * w_ref[2:3, :] + b_ref[...]
    y = _silu(y)
    col = lax.broadcasted_iota(jnp.int32, x.shape, 1)
    o_ref[...] = jnp.where(col >= B_W, y * (ML_DH ** -0.5), y).astype(BF16)


def _ml_prep(p_b, conv_w, conv_b, tr, edge_flags):
    r = p_b.shape[0]
    w = 2 * B_W
    nb = r // 8
    return pl.pallas_call(
        _ml_prep_kernel,
        grid=(r // tr,),
        in_specs=[pl.BlockSpec((tr, w), lambda i: (i, 0)),
                  pl.BlockSpec((8, w), lambda i: (jnp.maximum(i * (tr // 8) - 1, 0), 0)),
                  pl.BlockSpec((8, w), lambda i: (jnp.minimum((i + 1) * (tr // 8), nb - 1), 0)),
                  pl.BlockSpec((8, w), lambda i: (0, 0)),
                  pl.BlockSpec((1, w), lambda i: (0, 0)),
                  pl.BlockSpec(memory_space=pltpu.SMEM)],
        out_specs=pl.BlockSpec((tr, w), lambda i: (i, 0)),
        out_shape=jax.ShapeDtypeStruct((r, w), BF16),
        compiler_params=_cp("arbitrary"),
        name="mlstm_prep",
    )(p_b, p_b, p_b, jnp.pad(conv_w, ((0, 8 - conv_w.shape[0]), (0, 0))), conv_b.reshape(1, w),
      edge_flags)


def _ml_scan_kernel(qk_ref, v_ref, gi_ref, gf_ref, git_ref, gft_ref, tri_ref, y_ref,
                    c_ref, n_ref, m_ref):
    @pl.when(pl.program_id(2) == 0)
    def _():
        c_ref[...] = jnp.zeros(c_ref.shape, F32)
        n_ref[...] = jnp.zeros(n_ref.shape, F32)
        m_ref[...] = jnp.zeros(m_ref.shape, F32)

    tri = tri_ref[0]
    causal = tri > 0.0
    lf = _log_sigmoid(gf_ref[0])
    lft = _log_sigmoid(gft_ref[0])
    bcol = jnp.dot(tri, lf, precision=HI, preferred_element_type=F32)
    brow = lax.dot_general(lft, tri, (((1,), (1,)), ((), ())), precision=HI,
                           preferred_element_type=F32)
    tot = jnp.sum(lf, axis=0, keepdims=True)
    gi = gi_ref[0]
    git = git_ref[0]
    for h in range(ML_HEADS):
        sl = slice(h * ML_DH, (h + 1) * ML_DH)
        q = qk_ref[:, sl]
        k = qk_ref[:, B_W + h * ML_DH:B_W + (h + 1) * ML_DH]
        v = v_ref[:, sl]
        bc = bcol[:, h:h + 1]
        ic = gi[:, h:h + 1]
        br = brow[h:h + 1, :]
        ir = git[h:h + 1, :]
        m_prev = m_ref[h]
        dlog = jnp.where(causal, bc - br + ir, NEG)
        inter = bc + m_prev
        m = jnp.maximum(inter, jnp.max(dlog, axis=1, keepdims=True))
        w_inter = jnp.exp(inter - m)
        qk = _nt(q, k) * jnp.exp(dlog - m)
        cm = c_ref[h]
        nm = n_ref[h]
        num = (jnp.dot(qk.astype(BF16), v, preferred_element_type=F32)
               + w_inter * jnp.dot(q, cm.astype(BF16), preferred_element_type=F32))
        den = (jnp.sum(qk, axis=1, keepdims=True)
               + w_inter * jnp.sum(q.astype(F32) * nm, axis=1, keepdims=True))
        y_ref[0, :, sl] = num / jnp.maximum(jnp.abs(den), jnp.exp(-m))
        t_h = tot[:, h:h + 1]
        log_w = t_h - bc + ic
        m_new = jnp.maximum(t_h + m_prev, jnp.max(log_w, axis=0, keepdims=True))
        decay = jnp.exp(t_h + m_prev - m_new)
        kw = k.astype(F32) * jnp.exp(log_w - m_new)
        c_ref[h] = decay * cm + _tn(kw.astype(BF16), v)
        n_ref[h] = decay * nm + jnp.sum(kw, axis=0, keepdims=True)
        m_ref[h] = m_new


def _ml_scan(qk, p_b, gi, gf, git, gft, bsz, seq, ctx_len):
    r = qk.shape[0]
    c = ML_CHUNK
    idx = np.arange(c)
    tri = np.stack([(idx[None, :] <= idx[:, None]), (idx[None, :] >= idx[:, None])]).astype(np.float32)
    n_lat, n_ctx = seq // c, ctx_len // c
    blk = _chunk_index(bsz, n_lat, n_ctx, 0, bsz * n_lat)
    h = ML_HEADS
    return pl.pallas_call(
        _ml_scan_kernel,
        grid=(bsz, 2, n_lat + n_ctx),
        in_specs=[pl.BlockSpec((c, 2 * B_W), lambda b, d, i: (blk(b, d, i), 0)),
                  pl.BlockSpec((c, B_W), lambda b, d, i: (blk(b, d, i), 2)),
                  pl.BlockSpec((1, c, h), lambda b, d, i: (d, blk(b, d, i), 0)),
                  pl.BlockSpec((1, c, h), lambda b, d, i: (d, blk(b, d, i), 0)),
                  pl.BlockSpec((1, h, c), lambda b, d, i: (d, 0, blk(b, d, i))),
                  pl.BlockSpec((1, h, c), lambda b, d, i: (d, 0, blk(b, d, i))),
                  pl.BlockSpec((1, c, c), lambda b, d, i: (d, 0, 0))],
        out_specs=pl.BlockSpec((1, c, B_W), lambda b, d, i: (d, blk(b, d, i), 0)),
        out_shape=jax.ShapeDtypeStruct((2, r, B_W), F32),
        scratch_shapes=[pltpu.VMEM((h, ML_DH, ML_DH), F32),
                        pltpu.VMEM((h, 1, ML_DH), F32),
                        pltpu.VMEM((h, 1, 1), F32)],
        compiler_params=_cp("arbitrary", "arbitrary", "arbitrary"),
        name="mlstm_scan",
    )(qk, p_b, gi, gf, git, gft, jnp.asarray(tri))


def _post_kernel(yf_ref, yb_ref, za_ref, g_ref, hb_ref, zb_ref, ob_ref, oa_ref, obb_ref):
    for h in range(GLA_HEADS):
        sl = slice(h * GLA_DV, (h + 1) * GLA_DV)
        o = yf_ref[:, sl] + yb_ref[:, sl]
        o = o * lax.rsqrt(jnp.mean(o * o, axis=1, keepdims=True) + EPS) * g_ref[:, sl]
        oa_ref[:, sl] = (o * _silu(za_ref[:, sl].astype(F32))).astype(BF16)
    hb = hb_ref[0] + hb_ref[1]
    obb_ref[...] = (jax.nn.sigmoid(ob_ref[...].astype(F32)) * hb
                    * _silu(zb_ref[...].astype(F32))).astype(BF16)


def _post(y_f, y_b, p_a, gnorm, h_b, p_b, tr):
    r = p_a.shape[0]
    za_col = (2 * A_QK + A_V) // A_V
    return pl.pallas_call(
        _post_kernel,
        grid=(r // tr,),
        in_specs=[pl.BlockSpec((tr, A_V), lambda i: (i, 0)),
                  pl.BlockSpec((tr, A_V), lambda i: (i, 0)),
                  pl.BlockSpec((tr, A_V), lambda i: (i, za_col)),
                  pl.BlockSpec((1, A_V), lambda i: (0, 0)),
                  pl.BlockSpec((2, tr, B_W), lambda i: (0, i, 0)),
                  pl.BlockSpec((tr, B_W), lambda i: (i, 3)),
                  pl.BlockSpec((tr, B_W), lambda i: (i, 4))],
        out_specs=[pl.BlockSpec((tr, A_V), lambda i: (i, 0)),
                   pl.BlockSpec((tr, B_W), lambda i: (i, 0))],
        out_shape=[jax.ShapeDtypeStruct((r, A_V), BF16),
                   jax.ShapeDtypeStruct((r, B_W), BF16)],
        compiler_params=_cp("arbitrary"),
        name="post_ab",
    )(y_f, y_b, p_a, gnorm.reshape(1, A_V), h_b, p_b, p_b)


def _na_kernel(q_ref, k_ref, v_ref, z_ref, kc_ref, vc_ref, t_ref, o_ref, s_ref, p_ref, *, rows):
    kc = kc_ref[...]
    vc = vc_ref[...]
    win = NA_KH * GRID_W
    nq = NA_ROWS * GRID_W

    def body(t, carry):
        q0 = pl.multiple_of(t * nq, nq)
        q = (q_ref[pl.ds(q0, nq), :].astype(F32) * (NA_DH ** -0.5)).astype(BF16)
        starts = []
        for j in range(NA_ROWS):
            r = t * NA_ROWS + j
            rs = jnp.clip(r - NA_KH // 2, 0, rows - NA_KH)
            k0 = pl.multiple_of(rs * GRID_W, GRID_W)
            starts.append(k0)
            qs = slice(j * GRID_W, (j + 1) * GRID_W)
            s_ref[qs, :] = _nt(q[qs], k_ref[pl.ds(k0, win), :]) + t_ref[0, r - rs]
        s_c = _nt(q, kc)
        s_w = s_ref[...]
        m = jnp.maximum(jnp.max(s_w, axis=1, keepdims=True), jnp.max(s_c, axis=1, keepdims=True))
        p_w = jnp.exp(s_w - m)
        p_c = jnp.exp(s_c - m)
        l = jnp.sum(p_w, axis=1, keepdims=True) + jnp.sum(p_c, axis=1, keepdims=True)
        p_ref[...] = p_w.astype(BF16)
        o_c = jnp.dot(p_c.astype(BF16), vc, preferred_element_type=F32)
        o_w = jnp.concatenate(
            [jnp.dot(p_ref[j * GRID_W:(j + 1) * GRID_W, :], v_ref[pl.ds(starts[j], win), :],
                     preferred_element_type=F32) for j in range(NA_ROWS)], axis=0)
        z = z_ref[pl.ds(q0, nq), :].astype(F32)
        o_ref[pl.ds(q0, nq), :] = ((o_w + o_c) / l * _silu(z)).astype(BF16)
        return carry

    lax.fori_loop(0, rows // NA_ROWS, body, 0)


def _na_bias_table(rpb):
    c = np.arange(GRID_W)
    cs = np.clip(c - NA_KW // 2, 0, GRID_W - NA_KW)
    valid = (c[None, :] >= cs[:, None]) & (c[None, :] < cs[:, None] + NA_KW)
    dc = c[None, :] - c[:, None] + NA_KW - 1
    ndc = 2 * NA_KW - 1
    onehot = ((dc[None] == np.arange(ndc)[:, None, None]) & valid[None]).astype(np.float32)
    full = jnp.einsum("hdj,jck->hdck", rpb, jnp.asarray(onehot), precision=HI)
    full = full + jnp.asarray(np.where(valid, 0.0, NEG).astype(np.float32))
    t = jnp.stack([full[:, NA_KH - 1 - oi:2 * NA_KH - 1 - oi] for oi in range(NA_KH)], axis=1)
    t = jnp.transpose(t, (0, 1, 3, 2, 4))
    return t.reshape(NA_HEADS, NA_KH, GRID_W, NA_KH * GRID_W)


def _na(p_c, table, bsz, seq, ctx_len):
    r = p_c.shape[0]
    rows = seq // GRID_W
    ctx_blk0 = (bsz * seq) // ctx_len
    hq, hk, hv, hz = 0, NA_HEADS, 2 * NA_HEADS, 3 * NA_HEADS
    lat = lambda off: pl.BlockSpec((seq, NA_DH), lambda b, h: (b, off + h))
    ctx = lambda off: pl.BlockSpec((ctx_len, NA_DH), lambda b, h: (ctx_blk0 + b, off + h))
    return pl.pallas_call(
        functools.partial(_na_kernel, rows=rows),
        grid=(bsz, NA_HEADS),
        in_specs=[lat(hq), lat(hk), lat(hv), lat(hz), ctx(hk), ctx(hv),
                  pl.BlockSpec((1,) + table.shape[1:], lambda b, h: (h, 0, 0, 0))],
        out_specs=pl.BlockSpec((seq, NA_DH), lambda b, h: (b, h)),
        out_shape=jax.ShapeDtypeStruct((r, C_W), BF16),
        scratch_shapes=[pltpu.VMEM((NA_ROWS * GRID_W, NA_KH * GRID_W), F32),
                        pltpu.VMEM((NA_ROWS * GRID_W, NA_KH * GRID_W), BF16)],
        compiler_params=_cp("arbitrary", "arbitrary"),
        name="nbr_attn",
    )(p_c, p_c, p_c, p_c, p_c, p_c, table)


def _ctx_attn_kernel(q_ref, k_ref, v_ref, z_ref, y_in_ref, o_ref):
    del y_in_ref
    q = (q_ref[...].astype(F32) * (NA_DH ** -0.5)).astype(BF16)
    s = _nt(q, k_ref[...])
    m = jnp.max(s, axis=1, keepdims=True)
    p = jnp.exp(s - m)
    o = jnp.dot(p.astype(BF16), v_ref[...], preferred_element_type=F32) / jnp.sum(p, axis=1, keepdims=True)
    o_ref[...] = (o * _silu(z_ref[...].astype(F32))).astype(BF16)


def _ctx_attn(p_c, y_c, bsz, seq, ctx_len):
    ctx_blk0 = (bsz * seq) // ctx_len
    spec = lambda off: pl.BlockSpec((ctx_len, NA_DH), lambda b, h: (ctx_blk0 + b, off + h))
    return pl.pallas_call(
        _ctx_attn_kernel,
        grid=(bsz, NA_HEADS),
        in_specs=[spec(0), spec(NA_HEADS), spec(2 * NA_HEADS), spec(3 * NA_HEADS),
                  pl.BlockSpec(memory_space=pl.ANY)],
        out_specs=spec(0),
        out_shape=jax.ShapeDtypeStruct(y_c.shape, y_c.dtype),
        input_output_aliases={4: 0},
        compiler_params=_cp("arbitrary", "arbitrary"),
        name="ctx_attn",
    )(p_c, p_c, p_c, p_c, y_c)


def _rope_tables(seq, ctx_len):
    nf = GLA_DK // 4
    t = np.arange(seq)
    inv = ROPE_BASE ** (-np.arange(nf, dtype=np.float32) / nf)
    ar = (t // GRID_W).astype(np.float32)[:, None] * inv[None, :]
    ac = (t % GRID_W).astype(np.float32)[:, None] * inv[None, :]
    ar, ac = jnp.asarray(ar), jnp.asarray(ac)
    cos = jnp.concatenate([jnp.cos(ar), jnp.cos(ar), jnp.cos(ac), jnp.cos(ac)], axis=1)
    sin = jnp.concatenate([-jnp.sin(ar), jnp.sin(ar), -jnp.sin(ac), jnp.sin(ac)], axis=1)
    cos = jnp.concatenate([cos, jnp.ones((ctx_len, GLA_DK), F32)], axis=0)
    sin = jnp.concatenate([sin, jnp.zeros((ctx_len, GLA_DK), F32)], axis=0)
    pad = jnp.zeros((seq + ctx_len, LANE - GLA_DK), F32)
    cos = jnp.tile(jnp.concatenate([cos, pad], axis=1), (1, GLA_HEADS))
    sin = jnp.tile(jnp.concatenate([sin, pad], axis=1), (1, GLA_HEADS))
    return cos, sin


def _rope_mats():
    e = np.zeros((A_QK, GLA_PAD), np.float32)
    pe = np.zeros((A_QK, GLA_PAD), np.float32)
    quarter = GLA_DK // 4
    for i in range(A_QK):
        h, j = divmod(i, GLA_DK)
        e[i, h * LANE + j] = 1.0
    for h in range(GLA_HEADS):
        for j in range(GLA_DK):
            src = j + quarter if (j % (2 * quarter)) < quarter else j - quarter
            pe[h * GLA_DK + src, h * LANE + j] = 1.0
    return jnp.asarray(e, BF16), jnp.asarray(pe, BF16)


def _alpha_weights(w_alpha2, b_alpha):
    w = jnp.zeros((LANE, 2, GLA_HEADS, LANE), F32)
    b = jnp.zeros((2, GLA_HEADS, LANE), F32)
    for r in range(2):
        wr = w_alpha2[r].reshape(GLA_RANK, GLA_HEADS, GLA_DK)
        w = w.at[r * GLA_RANK:(r + 1) * GLA_RANK, r, :, :GLA_DK].set(wr)
        b = b.at[r, :, :GLA_DK].set(b_alpha[r].reshape(GLA_HEADS, GLA_DK))
    return w.reshape(LANE, 2 * GLA_PAD), b.reshape(1, 2 * GLA_PAD)


def kernel(x, c, ctx, c_ctx, w_mod, b_mod, norm_g, w_in, w_alpha2, b_alpha, gla_norm_g, b_gates,
           conv_w, conv_b, rpb, w_merge, b_merge, w_proj_a, w_proj_b, w_proj_c, w_out, final_g):
    bsz, seq, d = x.shape
    ctx_len = ctx.shape[1]
    depth = w_mod.shape[0]
    n_lat_rows = bsz * seq
    r_all = n_lat_rows + bsz * ctx_len
    tr = min(256, ctx_len)
    assert bsz < 8 and seq % tr == 0 and ctx_len % tr == 0 and tr % ML_CHUNK == 0
    assert seq % (GRID_W * NA_ROWS) == 0 and seq // GRID_W >= NA_KH and n_lat_rows % ctx_len == 0

    lat_tiles = seq // tr
    group_tr = lambda i: jnp.minimum(i // lat_tiles, bsz)
    n_lat_t, ctx_tiles = n_lat_rows // tr, ctx_len // tr
    pos_tile = lambda i: jnp.where(i < n_lat_t, i % lat_tiles, lat_tiles + (i - n_lat_t) % ctx_tiles)
    ti = np.arange(r_all // tr)
    in_lat = ti < n_lat_t
    first = np.where(in_lat, ti % lat_tiles == 0, (ti - n_lat_t) % ctx_tiles == 0)
    last = np.where(in_lat, ti % lat_tiles == lat_tiles - 1, (ti - n_lat_t) % ctx_tiles == ctx_tiles - 1)
    edge_flags = jnp.asarray(np.stack([first, last], axis=1).astype(np.int32))

    cc = jnp.zeros((8, d), F32).at[:bsz].set(c).at[bsz].set(c_ctx)
    mod = _modulation(cc, w_mod, b_mod)
    cos_t, sin_t = _rope_tables(seq, ctx_len)
    e_mat, pe_mat = _rope_mats()
    w_in_t = jnp.swapaxes(w_in, 1, 2)

    xall = jnp.concatenate([x.reshape(n_lat_rows, d), ctx.reshape(bsz * ctx_len, d)], axis=0)
    for l in range(depth):
        last_layer = l == depth - 1
        shift = mod[l, :, :d].reshape(8, 1, d)
        scale = mod[l, :, d:2 * d].reshape(8, 1, d)
        gate = mod[l, :, 2 * d:]
        w_small = jnp.zeros((LANE, d), F32)
        w_small = w_small.at[:2 * GLA_RANK].set(w_in_t[l, OFF_AA:OFF_B])
        w_small = w_small.at[2 * GLA_RANK:2 * GLA_RANK + 4 * ML_HEADS].set(w_in_t[l, OFF_GATES:OFF_C])
        h, small = _norm_mod(xall, norm_g[l], shift, scale, w_small.T, tr, group_tr)

        p_a = _mm_t(h, w_in_t, l, 0, OFF_AA, r_all, name="in_proj_a")
        p_b = _mm_t(h, w_in_t, l, OFF_B, 5 * B_W, r_all, name="in_proj_b")
        p_c = _mm_t(h, w_in_t, l, OFF_C, 4 * C_W, r_all, name="in_proj_c")

        w2pad, b2pad = _alpha_weights(w_alpha2[l], b_alpha[l])
        qr, kr, bcum = _gla_prep(p_a, small, w2pad, b2pad, e_mat, pe_mat, cos_t, sin_t, tr, pos_tile)
        y_f = _gla_scan(qr, kr, p_a, bcum, bsz, seq, ctx_len, rev=False)
        y_b = _gla_scan(qr, kr, p_a, bcum, bsz, seq, ctx_len, rev=True)

        qk_b = _ml_prep(p_b, conv_w[l], conv_b[l], tr, edge_flags)
        g = small[:, 2 * GLA_RANK:2 * GLA_RANK + 4 * ML_HEADS].reshape(r_all, 4, ML_HEADS) + b_gates[l]
        gi = jnp.stack([g[:, 0], g[:, 2]])
        gf = jnp.stack([g[:, 1], g[:, 3]])
        h_b = _ml_scan(qk_b, p_b, gi, gf, jnp.swapaxes(gi, 1, 2), jnp.swapaxes(gf, 1, 2),
                       bsz, seq, ctx_len)
        ya, yb = _post(y_f, y_b, p_a, gla_norm_g[l], h_b, p_b, tr)

        yc = _na(p_c, _na_bias_table(rpb[l]), bsz, seq, ctx_len)
        if not last_layer:
            yc = _ctx_attn(p_c, yc, bsz, seq, ctx_len)

        n_rows = n_lat_rows if last_layer else r_all
        gates = _mm_gate(h, w_merge, b_merge.reshape(depth, 1, -1), l, n_rows, name="merge_gates")
        u = _proj_merge(ya, yb, yc, w_proj_a, w_proj_b, w_proj_c, l, gates, n_rows)
        xall = _out_proj(u, w_out, l, xall, gate, n_rows, bsz, seq)

    out = _final_norm(xall, final_g, n_lat_rows, tr)
    return out.reshape(bsz, seq, d)
```

```python
import functools

import numpy as np
import jax
import jax.numpy as jnp
from jax import lax
from jax.experimental import pallas as pl
from jax.experimental.pallas import tpu as pltpu

F32 = jnp.float32
BF16 = jnp.bfloat16
HI = lax.Precision.HIGHEST

GRID_W = 64
EPS = 1e-6
ROPE_BASE = 10000.0
GLA_HEADS, GLA_DK, GLA_DV, GLA_RANK, GLA_TAU = 8, 64, 128, 16, 16.0
ML_HEADS, ML_DH = 8, 128
NA_HEADS, NA_DH, NA_KH, NA_KW = 16, 128, 8, 16
A_QK = GLA_HEADS * GLA_DK
A_V = GLA_HEADS * GLA_DV
B_W = ML_HEADS * ML_DH
C_W = NA_HEADS * NA_DH
LANE = 128
GLA_PAD = GLA_HEADS * LANE
GLA_CHUNK = 64
GLA_SUB = 16
ML_CHUNK = 128
NA_ROWS = 8
NEG = -1e30
OFF_AA = 2 * A_QK + 2 * A_V
OFF_B = OFF_AA + 2 * GLA_RANK
OFF_GATES = OFF_B + 5 * B_W
OFF_C = OFF_GATES + 4 * ML_HEADS
VMEM_LIMIT = 52 * 1024 * 1024


def _cp(*sem):
    return pltpu.CompilerParams(dimension_semantics=sem, vmem_limit_bytes=VMEM_LIMIT)


def _nt(a, b):
    return lax.dot_general(a, b, (((1,), (1,)), ((), ())), preferred_element_type=F32)


def _tn(a, b):
    return lax.dot_general(a, b, (((0,), (0,)), ((), ())), preferred_element_type=F32)


def _dot_split(a, b):
    a_hi = a.astype(BF16)
    a_lo = (a - a_hi.astype(F32)).astype(BF16)
    b_hi = b.astype(BF16)
    b_lo = (b - b_hi.astype(F32)).astype(BF16)
    d = lambda x, y: jnp.dot(x, y, preferred_element_type=F32)
    return d(a_hi, b_hi) + (d(a_hi, b_lo) + d(a_lo, b_hi))


def _log_sigmoid(x):
    return jnp.minimum(x, 0.0) - jnp.log1p(jnp.exp(-jnp.abs(x)))


def _silu(x):
    return x * jax.nn.sigmoid(x)


def _pick(n, cands):
    for c in cands:
        if n % c == 0:
            return c
    raise ValueError(f"no tile for {n}")


def _mod_kernel(s_ref, w_ref, b_ref, o_ref, *, kc):
    s = _silu(s_ref[...]).astype(BF16)
    d = s.shape[1]
    acc = jnp.zeros(o_ref.shape[1:], F32)
    for k0 in range(0, d, kc):
        acc = acc + jnp.dot(s[:, k0:k0 + kc], w_ref[0, k0:k0 + kc, :].astype(BF16),
                            preferred_element_type=F32)
    o_ref[0] = acc + b_ref[0]


def _modulation(cc, w_mod, b_mod):
    depth, d, n = w_mod.shape
    tn = _pick(n, (512, 256, 128))
    kc = _pick(d, (512, 256, 128))
    return pl.pallas_call(
        functools.partial(_mod_kernel, kc=kc),
        grid=(depth, n // tn),
        in_specs=[pl.BlockSpec((8, d), lambda l, j: (0, 0)),
                  pl.BlockSpec((1, d, tn), lambda l, j: (l, 0, j)),
                  pl.BlockSpec((1, 1, tn), lambda l, j: (l, 0, j))],
        out_specs=pl.BlockSpec((1, 8, tn), lambda l, j: (l, 0, j)),
        out_shape=jax.ShapeDtypeStruct((depth, 8, n), F32),
        compiler_params=_cp("arbitrary", "arbitrary"),
        name="modulation",
    )(cc, w_mod, b_mod.reshape(depth, 1, n))


def _norm_kernel(x_ref, g_ref, sh_ref, sc_ref, ws_ref, h_ref, sm_ref):
    x = x_ref[...]
    xn = x * lax.rsqrt(jnp.mean(x * x, axis=1, keepdims=True) + EPS)
    h = xn * g_ref[...] * (1.0 + sc_ref[0]) + sh_ref[0]
    h_ref[...] = h.astype(BF16)
    sm_ref[...] = _dot_split(h, ws_ref[...])


def _norm_mod(xall, g, shift, scale, w_small, tr, group_of):
    r, d = xall.shape
    return pl.pallas_call(
        _norm_kernel,
        grid=(r // tr,),
        in_specs=[pl.BlockSpec((tr, d), lambda i: (i, 0)),
                  pl.BlockSpec((1, d), lambda i: (0, 0)),
                  pl.BlockSpec((1, 1, d), lambda i: (group_of(i), 0, 0)),
                  pl.BlockSpec((1, 1, d), lambda i: (group_of(i), 0, 0)),
                  pl.BlockSpec((d, LANE), lambda i: (0, 0))],
        out_specs=[pl.BlockSpec((tr, d), lambda i: (i, 0)),
                   pl.BlockSpec((tr, LANE), lambda i: (i, 0))],
        out_shape=[jax.ShapeDtypeStruct((r, d), BF16),
                   jax.ShapeDtypeStruct((r, LANE), F32)],
        compiler_params=_cp("arbitrary"),
        name="norm_mod",
    )(xall, g.reshape(1, d), shift, scale, w_small)


def _final_norm_kernel(x_ref, g_ref, o_ref):
    x = x_ref[...]
    o_ref[...] = x * lax.rsqrt(jnp.mean(x * x, axis=1, keepdims=True) + EPS) * g_ref[...]


def _final_norm(x, g, n_rows, tr):
    d = x.shape[1]
    return pl.pallas_call(
        _final_norm_kernel,
        grid=(n_rows // tr,),
        in_specs=[pl.BlockSpec((tr, d), lambda i: (i, 0)),
                  pl.BlockSpec((1, d), lambda i: (0, 0))],
        out_specs=pl.BlockSpec((tr, d), lambda i: (i, 0)),
        out_shape=jax.ShapeDtypeStruct((n_rows, d), F32),
        compiler_params=_cp("arbitrary"),
        name="final_norm",
    )(x, g.reshape(1, d))


def _row_tile(n_rows):
    for t in range(1152, 127, -16):
        if n_rows % t == 0:
            return t
    raise ValueError(f"no row tile for {n_rows}")


def _cast_weight(w_ref, wb_ref):
    k = wb_ref.shape[0]
    kc = _pick(k, (512, 256, 128))
    for k0 in range(0, k, kc):
        wb_ref[k0:k0 + kc, :] = w_ref[0, k0:k0 + kc, :].astype(BF16)


def _mm_gate_kernel(x_ref, w_ref, b_ref, o_ref, wb_ref):
    @pl.when(pl.program_id(1) == 0)
    def _():
        _cast_weight(w_ref, wb_ref)

    acc = jnp.dot(x_ref[...], wb_ref[...], preferred_element_type=F32)
    o_ref[...] = jax.nn.sigmoid(acc + b_ref[0]).astype(o_ref.dtype)


def _mm_gate(x, w, bias, layer, n_rows, name):
    k, n = w.shape[1:]
    tn = _pick(n, (512, 256, 128))
    tm = _row_tile(n_rows)
    return pl.pallas_call(
        _mm_gate_kernel,
        grid=(n // tn, n_rows // tm),
        in_specs=[pl.BlockSpec((tm, k), lambda j, i: (i, 0)),
                  pl.BlockSpec((1, k, tn), lambda j, i: (layer, 0, j)),
                  pl.BlockSpec((1, 1, tn), lambda j, i: (layer, 0, j))],
        out_specs=pl.BlockSpec((tm, tn), lambda j, i: (i, j)),
        out_shape=jax.ShapeDtypeStruct((n_rows, n), BF16),
        scratch_shapes=[pltpu.VMEM((k, tn), BF16)],
        compiler_params=_cp("arbitrary", "arbitrary"),
        name=name,
    )(x, w, bias)


def _mm_t_kernel(x_ref, w_ref, o_ref, wb_ref):
    @pl.when(pl.program_id(1) == 0)
    def _():
        for r0 in range(0, wb_ref.shape[0], LANE):
            wb_ref[r0:r0 + LANE, :] = w_ref[0, r0:r0 + LANE, :].astype(BF16)

    o_ref[...] = _nt(x_ref[...], wb_ref[...]).astype(o_ref.dtype)


def _mm_t(x, w_t, layer, row0, n, n_rows, name):
    k = w_t.shape[2]
    tn = _pick(n, (512, 256, 128))
    tm = _row_tile(n_rows)
    assert row0 % 8 == 0
    if row0 % tn == 0:
        wspec = pl.BlockSpec((1, tn, k), lambda j, i: (layer, row0 // tn + j, 0))
    else:
        wspec = pl.BlockSpec((pl.Element(1), pl.Element(tn), pl.Element(k)),
                             lambda j, i: (layer, pl.multiple_of(row0 + j * tn, 8), 0))
    return pl.pallas_call(
        _mm_t_kernel,
        grid=(n // tn, n_rows // tm),
        in_specs=[pl.BlockSpec((tm, k), lambda j, i: (i, 0)), wspec],
        out_specs=pl.BlockSpec((tm, tn), lambda j, i: (i, j)),
        out_shape=jax.ShapeDtypeStruct((n_rows, n), BF16),
        scratch_shapes=[pltpu.VMEM((tn, k), BF16)],
        compiler_params=_cp("arbitrary", "arbitrary"),
        name=name,
    )(x, w_t)


def _proj_kernel(ya_ref, yb_ref, yc_ref, wa_ref, wb_ref, wc_ref, ga_ref, gb_ref, gc_ref,
                 o_ref, wa_s, wb_s, wc_s):
    @pl.when(pl.program_id(1) == 0)
    def _():
        _cast_weight(wa_ref, wa_s)
        _cast_weight(wb_ref, wb_s)
        _cast_weight(wc_ref, wc_s)

    u = ga_ref[...].astype(F32) * jnp.dot(ya_ref[...], wa_s[...], preferred_element_type=F32)
    u = u + gb_ref[...].astype(F32) * jnp.dot(yb_ref[...], wb_s[...], preferred_element_type=F32)
    u = u + gc_ref[...].astype(F32) * jnp.dot(yc_ref[...], wc_s[...], preferred_element_type=F32)
    o_ref[...] = u.astype(o_ref.dtype)


def _proj_merge(ya, yb, yc, wa, wb, wc, layer, gates, n_rows):
    d = wa.shape[2]
    tn = _pick(d, (512, 256, 128))
    tm = _row_tile(n_rows)
    nj = d // tn
    row = lambda j, i: (i, 0)
    col = lambda j, i: (layer, 0, j)
    return pl.pallas_call(
        _proj_kernel,
        grid=(nj, n_rows // tm),
        in_specs=[pl.BlockSpec((tm, ya.shape[1]), row),
                  pl.BlockSpec((tm, yb.shape[1]), row),
                  pl.BlockSpec((tm, yc.shape[1]), row),
                  pl.BlockSpec((1, wa.shape[1], tn), col),
                  pl.BlockSpec((1, wb.shape[1], tn), col),
                  pl.BlockSpec((1, wc.shape[1], tn), col),
                  pl.BlockSpec((tm, tn), lambda j, i: (i, j)),
                  pl.BlockSpec((tm, tn), lambda j, i: (i, nj + j)),
                  pl.BlockSpec((tm, tn), lambda j, i: (i, 2 * nj + j))],
        out_specs=pl.BlockSpec((tm, tn), lambda j, i: (i, j)),
        out_shape=jax.ShapeDtypeStruct((n_rows, d), BF16),
        scratch_shapes=[pltpu.VMEM((wa.shape[1], tn), BF16),
                        pltpu.VMEM((wb.shape[1], tn), BF16),
                        pltpu.VMEM((wc.shape[1], tn), BF16)],
        compiler_params=_cp("arbitrary", "arbitrary"),
        name="proj_merge",
    )(ya, yb, yc, wa, wb, wc, gates, gates, gates)


def _out_kernel(u_ref, w_ref, x_ref, g_ref, o_ref, wb_ref, *, bsz, seq):
    @pl.when(pl.program_id(1) == 0)
    def _():
        _cast_weight(w_ref, wb_ref)

    acc = jnp.dot(u_ref[...], wb_ref[...], preferred_element_type=F32)
    tm = acc.shape[0]
    row = pl.program_id(1) * tm + lax.broadcasted_iota(jnp.int32, (tm, 1), 0)
    g = g_ref[bsz:bsz + 1, :]
    for b in reversed(range(bsz)):
        g = jnp.where(row < (b + 1) * seq, g_ref[b:b + 1, :], g)
    o_ref[...] = x_ref[...] + g * acc


def _out_proj(u, w, layer, xall, gate, n_rows, bsz, seq):
    k, n = w.shape[1:]
    tn = _pick(n, (512, 256, 128))
    tm = _row_tile(n_rows)
    return pl.pallas_call(
        functools.partial(_out_kernel, bsz=bsz, seq=seq),
        grid=(n // tn, n_rows // tm),
        in_specs=[pl.BlockSpec((tm, k), lambda j, i: (i, 0)),
                  pl.BlockSpec((1, k, tn), lambda j, i: (layer, 0, j)),
                  pl.BlockSpec((tm, tn), lambda j, i: (i, j)),
                  pl.BlockSpec((8, tn), lambda j, i: (0, j))],
        out_specs=pl.BlockSpec((tm, tn), lambda j, i: (i, j)),
        out_shape=jax.ShapeDtypeStruct((n_rows, n), F32),
        scratch_shapes=[pltpu.VMEM((k, tn), BF16)],
        compiler_params=_cp("arbitrary", "arbitrary"),
        name="out_proj",
    )(u, w, xall, gate)


def _gla_prep_kernel(qk_ref, sm_ref, w2_ref, b2_ref, e_ref, pe_ref, cos_ref, sin_ref, tri_ref,
                     q_out, k_out, b_out):
    z = _dot_split(sm_ref[...], w2_ref[...]) + b2_ref[...]
    la = _log_sigmoid(z) * (1.0 / GLA_TAU)
    for d in range(2):
        rest = la[:, d * GLA_PAD:(d + 1) * GLA_PAD]
        acc = jnp.zeros(rest.shape, F32)
        for _ in range(3):
            part = rest.astype(BF16)
            acc = acc + jnp.dot(tri_ref[d], part, preferred_element_type=F32)
            rest = rest - part.astype(F32)
        b_out[d] = acc
    cos = cos_ref[...]
    sin = sin_ref[...]
    e = e_ref[...]
    pe = pe_ref[...]
    q = qk_ref[:, :A_QK]
    k = qk_ref[:, A_QK:]
    rope = lambda t: (jnp.dot(t, e, preferred_element_type=F32) * cos
                      + jnp.dot(t, pe, preferred_element_type=F32) * sin)
    q_out[...] = rope(q) * (GLA_DK ** -0.5)
    k_out[...] = rope(k)


def _gla_prep(p_a, small, w2pad, b2pad, e_mat, pe_mat, cos_t, sin_t, tr, pos_tile):
    r = p_a.shape[0]
    full = lambda i: (0, 0)
    idx = np.arange(tr)
    same_chunk = (idx[None, :] // GLA_CHUNK) == (idx[:, None] // GLA_CHUNK)
    tri = np.stack([same_chunk & (idx[None, :] <= idx[:, None]),
                    same_chunk & (idx[None, :] >= idx[:, None])]).astype(np.float32)
    tri = jnp.asarray(tri, BF16)
    return pl.pallas_call(
        _gla_prep_kernel,
        grid=(r // tr,),
        in_specs=[pl.BlockSpec((tr, 2 * A_QK), lambda i: (i, 0)),
                  pl.BlockSpec((tr, LANE), lambda i: (i, 0)),
                  pl.BlockSpec(w2pad.shape, full),
                  pl.BlockSpec(b2pad.shape, full),
                  pl.BlockSpec(e_mat.shape, full),
                  pl.BlockSpec(pe_mat.shape, full),
                  pl.BlockSpec((tr, GLA_PAD), lambda i: (pos_tile(i), 0)),
                  pl.BlockSpec((tr, GLA_PAD), lambda i: (pos_tile(i), 0)),
                  pl.BlockSpec(tri.shape, lambda i: (0, 0, 0))],
        out_specs=[pl.BlockSpec((tr, GLA_PAD), lambda i: (i, 0)),
                   pl.BlockSpec((tr, GLA_PAD), lambda i: (i, 0)),
                   pl.BlockSpec((2, tr, GLA_PAD), lambda i: (0, i, 0))],
        out_shape=[jax.ShapeDtypeStruct((r, GLA_PAD), F32),
                   jax.ShapeDtypeStruct((r, GLA_PAD), F32),
                   jax.ShapeDtypeStruct((2, r, GLA_PAD), F32)],
        compiler_params=_cp("arbitrary"),
        name="gla_prep",
    )(p_a, small, w2pad, b2pad, e_mat, pe_mat, cos_t, sin_t, tri)


def _gla_masks(rev):
    c = GLA_CHUNK
    idx = np.arange(c)
    pos = c - 1 - idx if rev else idx
    rb = pos // GLA_SUB
    rh = pos // (2 * GLA_SUB)
    md = (pos[None, :] <= pos[:, None]) & (rb[None, :] == rb[:, None])
    m16 = (rh[None, :] == rh[:, None]) & (rb[None, :] < rb[:, None])
    return np.stack([md, m16]).astype(np.float32)


def _gla_scan_kernel(q_ref, k_ref, v_ref, b_ref, mask_ref, y_ref, st_ref, dec_s, att_s, yi_s,
                     *ops_s, rev):
    sub = GLA_SUB
    nb = GLA_CHUNK // sub
    assert nb == 4

    @pl.when(pl.program_id(1) == 0)
    def _():
        st_ref[...] = jnp.zeros(st_ref.shape, F32)

    md = mask_ref[0] > 0.0
    m16 = mask_ref[1]
    rank = lambda kb: nb - 1 - kb if rev else kb
    by_rank = lambda rk: nb - 1 - rk if rev else rk
    rows = lambda kb: slice(kb * sub, (kb + 1) * sub)
    cat = lambda parts: jnp.concatenate(parts, axis=0)
    zeros = jnp.zeros((sub, LANE), BF16)
    heads = [slice(h * LANE, (h + 1) * LANE) for h in range(GLA_HEADS)]
    qd_s, kd_s, k16_s, q32_s, k32_s, qi_s, ka_s = ops_s
    for h, sl in enumerate(heads):
        b = b_ref[0, :, sl]
        q = q_ref[:, sl]
        k = k_ref[:, sl]

        def end(rk):
            kb = by_rank(rk)
            r0 = kb * sub if rev else kb * sub + sub - 1
            return b[r0:r0 + 1]

        tot = end(nb - 1)
        dq = cat([b[rows(kb)] - end(rank(kb) - 1) if rank(kb) > 0 else b[rows(kb)] for kb in range(nb)])
        qd_s[h] = (q * jnp.exp(dq)).astype(BF16)
        kd_s[h] = (k * jnp.exp(-dq)).astype(BF16)
        k16_s[h] = cat([(k[rows(kb)] * jnp.exp(end(rank(kb)) - b[rows(kb)])).astype(BF16)
                        if rank(kb) % 2 == 0 else zeros for kb in range(nb)])
        half_end = end(nb // 2 - 1)
        q32_s[h] = cat([(q[rows(kb)] * jnp.exp(b[rows(kb)] - half_end)).astype(BF16)
                        if rank(kb) >= nb // 2 else zeros for kb in range(nb)])
        k32_s[h] = cat([(k[rows(kb)] * jnp.exp(half_end - b[rows(kb)])).astype(BF16)
                        if rank(kb) < nb // 2 else zeros for kb in range(nb)])
        qi_s[h] = (q * jnp.exp(b)).astype(BF16)
        ka_s[h] = (k * jnp.exp(tot - b)).astype(BF16)
        dec_s[h] = jnp.exp(tot)
    for h, sl in enumerate(heads):
        qd = qd_s[h]
        att = (jnp.where(md, _nt(qd, kd_s[h]), 0.0) + m16 * _nt(qd, k16_s[h])
               + _nt(q32_s[h], k32_s[h]))
        att_s[h] = att.astype(BF16)
        yi_s[h] = _nt(qi_s[h], st_ref[h].astype(BF16))
    for h, sl in enumerate(heads):
        v = v_ref[:, sl]
        y_ref[:, sl] = jnp.dot(att_s[h], v, preferred_element_type=F32) + yi_s[h]
        st_ref[h] = st_ref[h] * dec_s[h] + _tn(v, ka_s[h])


def _chunk_index(bsz, n_lat, n_ctx, lat0, ctx0):
    def f(b, d, c):
        in_ctx = c < n_ctx
        cc = jnp.where(d == 0, c, n_ctx - 1 - c)
        cl = jnp.where(d == 0, c - n_ctx, n_lat - 1 - (c - n_ctx))
        return jnp.where(in_ctx, ctx0 + b * n_ctx + cc, lat0 + b * n_lat + cl)
    return f


def _gla_scan(qr, kr, p_a, bcum, bsz, seq, ctx_len, rev):
    r = qr.shape[0]
    c = GLA_CHUNK
    masks = jnp.asarray(_gla_masks(rev))
    n_lat, n_ctx = seq // c, ctx_len // c
    blk = _chunk_index(bsz, n_lat, n_ctx, 0, bsz * n_lat)
    d = int(rev)
    v_col = (2 * A_QK) // A_V
    return pl.pallas_call(
        functools.partial(_gla_scan_kernel, rev=rev),
        grid=(bsz, n_lat + n_ctx),
        in_specs=[pl.BlockSpec((c, GLA_PAD), lambda b, i: (blk(b, d, i), 0)),
                  pl.BlockSpec((c, GLA_PAD), lambda b, i: (blk(b, d, i), 0)),
                  pl.BlockSpec((c, A_V), lambda b, i: (blk(b, d, i), v_col)),
                  pl.BlockSpec((1, c, GLA_PAD), lambda b, i: (d, blk(b, d, i), 0)),
                  pl.BlockSpec(masks.shape, lambda b, i: (0, 0, 0))],
        out_specs=pl.BlockSpec((c, A_V), lambda b, i: (blk(b, d, i), 0)),
        out_shape=jax.ShapeDtypeStruct((r, A_V), F32),
        scratch_shapes=[pltpu.VMEM((GLA_HEADS, GLA_DV, LANE), F32),
                        pltpu.VMEM((GLA_HEADS, 1, LANE), F32),
                        pltpu.VMEM((GLA_HEADS, c, c), BF16),
                        pltpu.VMEM((GLA_HEADS, c, GLA_DV), F32)]
        + [pltpu.VMEM((GLA_HEADS, c, LANE), BF16)] * 7,
        compiler_params=_cp("arbitrary", "arbitrary"),
        name="gla_scan_bwd" if rev else "gla_scan_fwd",
    )(qr, kr, p_a, bcum, masks)


def _ml_prep_kernel(x_ref, xp_ref, xn_ref, w_ref, b_ref, first_ref, o_ref, kt_ref):
    i = pl.program_id(0)
    x = x_ref[...].astype(F32)
    tr = x.shape[0]
    is_first = first_ref[i, 0] > 0
    is_last = first_ref[i, 1] > 0
    prev = jnp.where(is_first, 0.0, xp_ref[7:8, :].astype(F32))
    nxt = jnp.where(is_last, 0.0, xn_ref[0:1, :].astype(F32))
    row = lax.broadcasted_iota(jnp.int32, x.shape, 0)
    xm1 = jnp.where(row == 0, prev, pltpu.roll(x, 1, axis=0))
    xp1 = jnp.where(row == tr - 1, nxt, pltpu.roll(x, tr - 1, axis=0))
    y = xm1 * w_ref[0:1, :] + x * w_ref[1:2, :] + xp1 * w_ref[2:3, :] + b_ref[...]
    y = _silu(y)
    k = y[:, B_W:] * (ML_DH ** -0.5)
    o_ref[:, :B_W] = y[:, :B_W].astype(BF16)
    o_ref[:, B_W:] = k.astype(BF16)
    kt_ref[...] = k.T.astype(BF16)


def _ml_prep(p_b, conv_w, conv_b, tr, edge_flags):
    r = p_b.shape[0]
    w = 2 * B_W
    nb = r // 8
    return pl.pallas_call(
        _ml_prep_kernel,
        grid=(r // tr,),
        in_specs=[pl.BlockSpec((tr, w), lambda i: (i, 0)),
                  pl.BlockSpec((8, w), lambda i: (jnp.maximum(i * (tr // 8) - 1, 0), 0)),
                  pl.BlockSpec((8, w), lambda i: (jnp.minimum((i + 1) * (tr // 8), nb - 1), 0)),
                  pl.BlockSpec((8, w), lambda i: (0, 0)),
                  pl.BlockSpec((1, w), lambda i: (0, 0)),
                  pl.BlockSpec(memory_space=pltpu.SMEM)],
        out_specs=[pl.BlockSpec((tr, w), lambda i: (i, 0)),
                   pl.BlockSpec((B_W, tr), lambda i: (0, i))],
        out_shape=[jax.ShapeDtypeStruct((r, w), BF16),
                   jax.ShapeDtypeStruct((B_W, r), BF16)],
        compiler_params=_cp("arbitrary"),
        name="mlstm_prep",
    )(p_b, p_b, p_b, jnp.pad(conv_w, ((0, 8 - conv_w.shape[0]), (0, 0))), conv_b.reshape(1, w),
      edge_flags)


def _ml_scan_kernel(qk_ref, kt_ref, v_ref, git_ref, gft_ref, tri_ref, y_ref, s_ref, m_ref,
                    qk_s, qs_s, pk_s, wi_s, kw_s):
    c = ML_CHUNK
    nh = ML_HEADS
    assert c == ML_DH
    d = pl.program_id(1)

    @pl.when(pl.program_id(2) == 0)
    def _():
        s_ref[...] = jnp.zeros(s_ref.shape, F32)
        m_ref[...] = jnp.zeros(m_ref.shape, F32)

    tri = tri_ref[0]
    causal = tri > 0.0
    lft = _log_sigmoid(gft_ref[0])
    brow = lax.dot_general(lft, tri, (((1,), (1,)), ((), ())), precision=HI,
                           preferred_element_type=F32)
    tot = jnp.sum(lft, axis=1, keepdims=True)
    g = git_ref[0] - brow
    lane = lax.broadcasted_iota(jnp.int32, (nh, c), 1)
    run = g
    sh = 1
    while sh < c:
        fwd = jnp.where(lane >= sh, pltpu.roll(run, sh, axis=1), NEG)
        bwd = jnp.where(lane < c - sh, pltpu.roll(run, c - sh, axis=1), NEG)
        run = jnp.maximum(run, jnp.where(d == 0, fwd, bwd))
        sh *= 2
    m_prev = m_ref[...]
    big_m = jnp.maximum(m_prev, run)
    m_next = jnp.maximum(m_prev, jnp.max(g, axis=1, keepdims=True))
    decay = jnp.exp(m_prev - m_next)
    w_rows = jnp.exp(g - m_next)
    cols = jnp.concatenate([big_m, brow + big_m, jnp.zeros((c - 2 * nh, c), F32)], axis=0).T
    ones = jnp.ones((c, ML_DH), BF16)
    heads = [slice(h * ML_DH, (h + 1) * ML_DH) for h in range(nh)]
    for h, sl in enumerate(heads):
        q = qk_ref[:, sl]
        qk_s[h] = jnp.dot(q, kt_ref[sl, :], preferred_element_type=F32)
        qs_s[h] = jnp.dot(q, s_ref[h].astype(BF16), preferred_element_type=F32)
    for h, sl in enumerate(heads):
        m_mat = jnp.broadcast_to(cols[:, h:h + 1], (c, c))
        p = jnp.where(causal, jnp.exp(g[h:h + 1, :] - m_mat), 0.0)
        pk_s[h] = (qk_s[h] * p).astype(BF16)
        wi_s[h] = jnp.exp(m_prev[h:h + 1, :] - m_mat)
        kw_s[h] = (kt_ref[sl, :].astype(F32) * w_rows[h:h + 1, :]).astype(BF16)
    for h, sl in enumerate(heads):
        vo = jnp.concatenate([v_ref[:, sl], ones], axis=1)
        w_inter = wi_s[h]
        nd = (jnp.dot(pk_s[h], vo, preferred_element_type=F32)
              + jnp.concatenate([w_inter, w_inter], axis=1) * qs_s[h])
        neg_m = jnp.exp(-jnp.broadcast_to(cols[:, nh + h:nh + h + 1], (c, ML_DH)))
        y_ref[0, :, sl] = nd[:, :ML_DH] / jnp.maximum(jnp.abs(nd[:, ML_DH:]), neg_m)
        s_ref[h] = decay[h:h + 1, :] * s_ref[h] + jnp.dot(kw_s[h], vo, preferred_element_type=F32)
    m_ref[...] = tot + m_next


def _ml_scan(qk, kt, p_b, git, gft, bsz, seq, ctx_len):
    r = qk.shape[0]
    c = ML_CHUNK
    idx = np.arange(c)
    tri = np.stack([(idx[None, :] <= idx[:, None]), (idx[None, :] >= idx[:, None])]).astype(np.float32)
    n_lat, n_ctx = seq // c, ctx_len // c
    blk = _chunk_index(bsz, n_lat, n_ctx, 0, bsz * n_lat)
    h = ML_HEADS
    return pl.pallas_call(
        _ml_scan_kernel,
        grid=(bsz, 2, n_lat + n_ctx),
        in_specs=[pl.BlockSpec((c, 2 * B_W), lambda b, d, i: (blk(b, d, i), 0)),
                  pl.BlockSpec((B_W, c), lambda b, d, i: (0, blk(b, d, i))),
                  pl.BlockSpec((c, B_W), lambda b, d, i: (blk(b, d, i), 2)),
                  pl.BlockSpec((1, h, c), lambda b, d, i: (d, 0, blk(b, d, i))),
                  pl.BlockSpec((1, h, c), lambda b, d, i: (d, 0, blk(b, d, i))),
                  pl.BlockSpec((1, c, c), lambda b, d, i: (d, 0, 0))],
        out_specs=pl.BlockSpec((1, c, B_W), lambda b, d, i: (d, blk(b, d, i), 0)),
        out_shape=jax.ShapeDtypeStruct((2, r, B_W), F32),
        scratch_shapes=[pltpu.VMEM((h, ML_DH, 2 * ML_DH), F32),
                        pltpu.VMEM((h, 1), F32),
                        pltpu.VMEM((h, c, c), F32),
                        pltpu.VMEM((h, c, 2 * ML_DH), F32),
                        pltpu.VMEM((h, c, c), BF16),
                        pltpu.VMEM((h, c, c), F32),
                        pltpu.VMEM((h, ML_DH, c), BF16)],
        compiler_params=_cp("arbitrary", "arbitrary", "arbitrary"),
        name="mlstm_scan",
    )(qk, kt, p_b, git, gft, jnp.asarray(tri))


def _post_kernel(yf_ref, yb_ref, za_ref, g_ref, hb_ref, zb_ref, ob_ref, oa_ref, obb_ref):
    for h in range(GLA_HEADS):
        sl = slice(h * GLA_DV, (h + 1) * GLA_DV)
        o = yf_ref[:, sl] + yb_ref[:, sl]
        o = o * lax.rsqrt(jnp.mean(o * o, axis=1, keepdims=True) + EPS) * g_ref[:, sl]
        oa_ref[:, sl] = (o * _silu(za_ref[:, sl].astype(F32))).astype(BF16)
    hb = hb_ref[0] + hb_ref[1]
    obb_ref[...] = (jax.nn.sigmoid(ob_ref[...].astype(F32)) * hb
                    * _silu(zb_ref[...].astype(F32))).astype(BF16)


def _post(y_f, y_b, p_a, gnorm, h_b, p_b, tr):
    r = p_a.shape[0]
    za_col = (2 * A_QK + A_V) // A_V
    return pl.pallas_call(
        _post_kernel,
        grid=(r // tr,),
        in_specs=[pl.BlockSpec((tr, A_V), lambda i: (i, 0)),
                  pl.BlockSpec((tr, A_V), lambda i: (i, 0)),
                  pl.BlockSpec((tr, A_V), lambda i: (i, za_col)),
                  pl.BlockSpec((1, A_V), lambda i: (0, 0)),
                  pl.BlockSpec((2, tr, B_W), lambda i: (0, i, 0)),
                  pl.BlockSpec((tr, B_W), lambda i: (i, 3)),
                  pl.BlockSpec((tr, B_W), lambda i: (i, 4))],
        out_specs=[pl.BlockSpec((tr, A_V), lambda i: (i, 0)),
                   pl.BlockSpec((tr, B_W), lambda i: (i, 0))],
        out_shape=[jax.ShapeDtypeStruct((r, A_V), BF16),
                   jax.ShapeDtypeStruct((r, B_W), BF16)],
        compiler_params=_cp("arbitrary"),
        name="post_ab",
    )(y_f, y_b, p_a, gnorm.reshape(1, A_V), h_b, p_b, p_b)


def _na_kernel(q_ref, k_ref, v_ref, z_ref, qc_ref, kc_ref, vc_ref, zc_ref, t_ref, o_ref,
               s_ref, p_ref, *, rows, bsz, ctx_len, with_ctx):
    b = pl.program_id(0)
    scale = NA_DH ** -0.5
    if with_ctx:
        @pl.when(b == bsz)
        def _():
            for bb in range(bsz):
                rs_ = slice(bb * ctx_len, (bb + 1) * ctx_len)
                q = (qc_ref[rs_, :].astype(F32) * scale).astype(BF16)
                s = _nt(q, kc_ref[rs_, :])
                p = jnp.exp(s - jnp.max(s, axis=1, keepdims=True))
                o = (jnp.dot(p.astype(BF16), vc_ref[rs_, :], preferred_element_type=F32)
                     / jnp.sum(p, axis=1, keepdims=True))
                o_ref[rs_, :] = (o * _silu(zc_ref[rs_, :].astype(F32))).astype(BF16)
            o_ref[bsz * ctx_len:, :] = jnp.zeros((o_ref.shape[0] - bsz * ctx_len, NA_DH), BF16)

    @pl.when(b < bsz)
    def _():
        _na_latent(q_ref, k_ref, v_ref, z_ref, kc_ref, vc_ref, t_ref, o_ref, s_ref, p_ref,
                   rows=rows, c0=pl.multiple_of(b * ctx_len, ctx_len), ctx_len=ctx_len)


def _na_latent(q_ref, k_ref, v_ref, z_ref, kc_ref, vc_ref, t_ref, o_ref, s_ref, p_ref, *,
               rows, c0, ctx_len):
    kc = kc_ref[pl.ds(c0, ctx_len), :]
    vc = vc_ref[pl.ds(c0, ctx_len), :]
    win = NA_KH * GRID_W
    nq = NA_ROWS * GRID_W

    def body(t, carry):
        q0 = pl.multiple_of(t * nq, nq)
        q = (q_ref[pl.ds(q0, nq), :].astype(F32) * (NA_DH ** -0.5)).astype(BF16)
        starts = []
        for j in range(NA_ROWS):
            r = t * NA_ROWS + j
            rs = jnp.clip(r - NA_KH // 2, 0, rows - NA_KH)
            k0 = pl.multiple_of(rs * GRID_W, GRID_W)
            starts.append(k0)
            qs = slice(j * GRID_W, (j + 1) * GRID_W)
            s_ref[qs, :] = _nt(q[qs], k_ref[pl.ds(k0, win), :]) + t_ref[0, r - rs]
        s_c = _nt(q, kc)
        s_w = s_ref[...]
        m = jnp.maximum(jnp.max(s_w, axis=1, keepdims=True), jnp.max(s_c, axis=1, keepdims=True))
        p_w = jnp.exp(s_w - m)
        p_c = jnp.exp(s_c - m)
        l = jnp.sum(p_w, axis=1, keepdims=True) + jnp.sum(p_c, axis=1, keepdims=True)
        p_ref[...] = p_w.astype(BF16)
        o_c = jnp.dot(p_c.astype(BF16), vc, preferred_element_type=F32)
        o_w = jnp.concatenate(
            [jnp.dot(p_ref[j * GRID_W:(j + 1) * GRID_W, :], v_ref[pl.ds(starts[j], win), :],
                     preferred_element_type=F32) for j in range(NA_ROWS)], axis=0)
        z = z_ref[pl.ds(q0, nq), :].astype(F32)
        o_ref[pl.ds(q0, nq), :] = ((o_w + o_c) / l * _silu(z)).astype(BF16)
        return carry

    lax.fori_loop(0, rows // NA_ROWS, body, 0)


def _na_bias_table(rpb):
    c = np.arange(GRID_W)
    cs = np.clip(c - NA_KW // 2, 0, GRID_W - NA_KW)
    valid = (c[None, :] >= cs[:, None]) & (c[None, :] < cs[:, None] + NA_KW)
    dc = c[None, :] - c[:, None] + NA_KW - 1
    ndc = 2 * NA_KW - 1
    onehot = ((dc[None] == np.arange(ndc)[:, None, None]) & valid[None]).astype(np.float32)
    full = jnp.einsum("hdj,jck->hdck", rpb, jnp.asarray(onehot), precision=HI)
    full = full + jnp.asarray(np.where(valid, 0.0, NEG).astype(np.float32))
    t = jnp.stack([full[:, NA_KH - 1 - oi:2 * NA_KH - 1 - oi] for oi in range(NA_KH)], axis=1)
    t = jnp.transpose(t, (0, 1, 3, 2, 4))
    return t.reshape(NA_HEADS, NA_KH, GRID_W, NA_KH * GRID_W)


def _na(p_c, table, bsz, seq, ctx_len, with_ctx):
    rows = seq // GRID_W
    n_ctx = bsz * ctx_len
    assert (bsz * seq) % n_ctx == 0 and n_ctx <= seq
    ctx_blk = (bsz * seq) // n_ctx
    hq, hk, hv, hz = 0, NA_HEADS, 2 * NA_HEADS, 3 * NA_HEADS
    lat = lambda off: pl.BlockSpec((seq, NA_DH), lambda b, h: (jnp.minimum(b, bsz - 1), off + h))
    ctx = lambda off: pl.BlockSpec((n_ctx, NA_DH), lambda b, h: (ctx_blk, off + h))
    nb = bsz + int(with_ctx)
    return pl.pallas_call(
        functools.partial(_na_kernel, rows=rows, bsz=bsz, ctx_len=ctx_len, with_ctx=with_ctx),
        grid=(nb, NA_HEADS),
        in_specs=[lat(hq), lat(hk), lat(hv), lat(hz), ctx(hq), ctx(hk), ctx(hv), ctx(hz),
                  pl.BlockSpec((1,) + table.shape[1:], lambda b, h: (h, 0, 0, 0))],
        out_specs=pl.BlockSpec((seq, NA_DH), lambda b, h: (b, h)),
        out_shape=jax.ShapeDtypeStruct((nb * seq, C_W), BF16),
        scratch_shapes=[pltpu.VMEM((NA_ROWS * GRID_W, NA_KH * GRID_W), F32),
                        pltpu.VMEM((NA_ROWS * GRID_W, NA_KH * GRID_W), BF16)],
        compiler_params=_cp("arbitrary", "arbitrary"),
        name="nbr_attn",
    )(p_c, p_c, p_c, p_c, p_c, p_c, p_c, p_c, table)


def _rope_tables(seq, ctx_len):
    nf = GLA_DK // 4
    t = np.arange(seq)
    inv = ROPE_BASE ** (-np.arange(nf, dtype=np.float32) / nf)
    ar = (t // GRID_W).astype(np.float32)[:, None] * inv[None, :]
    ac = (t % GRID_W).astype(np.float32)[:, None] * inv[None, :]
    ar, ac = jnp.asarray(ar), jnp.asarray(ac)
    cos = jnp.concatenate([jnp.cos(ar), jnp.cos(ar), jnp.cos(ac), jnp.cos(ac)], axis=1)
    sin = jnp.concatenate([-jnp.sin(ar), jnp.sin(ar), -jnp.sin(ac), jnp.sin(ac)], axis=1)
    cos = jnp.concatenate([cos, jnp.ones((ctx_len, GLA_DK), F32)], axis=0)
    sin = jnp.concatenate([sin, jnp.zeros((ctx_len, GLA_DK), F32)], axis=0)
    pad = jnp.zeros((seq + ctx_len, LANE - GLA_DK), F32)
    cos = jnp.tile(jnp.concatenate([cos, pad], axis=1), (1, GLA_HEADS))
    sin = jnp.tile(jnp.concatenate([sin, pad], axis=1), (1, GLA_HEADS))
    return cos, sin


def _rope_mats():
    e = np.zeros((A_QK, GLA_PAD), np.float32)
    pe = np.zeros((A_QK, GLA_PAD), np.float32)
    quarter = GLA_DK // 4
    for i in range(A_QK):
        h, j = divmod(i, GLA_DK)
        e[i, h * LANE + j] = 1.0
    for h in range(GLA_HEADS):
        for j in range(GLA_DK):
            src = j + quarter if (j % (2 * quarter)) < quarter else j - quarter
            pe[h * GLA_DK + src, h * LANE + j] = 1.0
    return jnp.asarray(e, BF16), jnp.asarray(pe, BF16)


def _alpha_weights(w_alpha2, b_alpha):
    w = jnp.zeros((LANE, 2, GLA_HEADS, LANE), F32)
    b = jnp.zeros((2, GLA_HEADS, LANE), F32)
    for r in range(2):
        wr = w_alpha2[r].reshape(GLA_RANK, GLA_HEADS, GLA_DK)
        w = w.at[r * GLA_RANK:(r + 1) * GLA_RANK, r, :, :GLA_DK].set(wr)
        b = b.at[r, :, :GLA_DK].set(b_alpha[r].reshape(GLA_HEADS, GLA_DK))
    return w.reshape(LANE, 2 * GLA_PAD), b.reshape(1, 2 * GLA_PAD)


def kernel(x, c, ctx, c_ctx, w_mod, b_mod, norm_g, w_in, w_alpha2, b_alpha, gla_norm_g, b_gates,
           conv_w, conv_b, rpb, w_merge, b_merge, w_proj_a, w_proj_b, w_proj_c, w_out, final_g):
    bsz, seq, d = x.shape
    ctx_len = ctx.shape[1]
    depth = w_mod.shape[0]
    n_lat_rows = bsz * seq
    r_all = n_lat_rows + bsz * ctx_len
    tr = min(256, ctx_len)
    assert bsz < 8 and seq % tr == 0 and ctx_len % tr == 0 and tr % ML_CHUNK == 0
    assert seq % (GRID_W * NA_ROWS) == 0 and seq // GRID_W >= NA_KH and n_lat_rows % ctx_len == 0

    lat_tiles = seq // tr
    group_tr = lambda i: jnp.minimum(i // lat_tiles, bsz)
    n_lat_t, ctx_tiles = n_lat_rows // tr, ctx_len // tr
    pos_tile = lambda i: jnp.where(i < n_lat_t, i % lat_tiles, lat_tiles + (i - n_lat_t) % ctx_tiles)
    ti = np.arange(r_all // tr)
    in_lat = ti < n_lat_t
    first = np.where(in_lat, ti % lat_tiles == 0, (ti - n_lat_t) % ctx_tiles == 0)
    last = np.where(in_lat, ti % lat_tiles == lat_tiles - 1, (ti - n_lat_t) % ctx_tiles == ctx_tiles - 1)
    edge_flags = jnp.asarray(np.stack([first, last], axis=1).astype(np.int32))

    cc = jnp.zeros((8, d), F32).at[:bsz].set(c).at[bsz].set(c_ctx)
    mod = _modulation(cc, w_mod, b_mod)
    cos_t, sin_t = _rope_tables(seq, ctx_len)
    e_mat, pe_mat = _rope_mats()
    w_in_t = jnp.swapaxes(w_in, 1, 2)

    xall = jnp.concatenate([x.reshape(n_lat_rows, d), ctx.reshape(bsz * ctx_len, d)], axis=0)
    for l in range(depth):
        last_layer = l == depth - 1
        shift = mod[l, :, :d].reshape(8, 1, d)
        scale = mod[l, :, d:2 * d].reshape(8, 1, d)
        gate = mod[l, :, 2 * d:]
        w_small = jnp.zeros((LANE, d), F32)
        w_small = w_small.at[:2 * GLA_RANK].set(w_in_t[l, OFF_AA:OFF_B])
        w_small = w_small.at[2 * GLA_RANK:2 * GLA_RANK + 4 * ML_HEADS].set(w_in_t[l, OFF_GATES:OFF_C])
        h, small = _norm_mod(xall, norm_g[l], shift, scale, w_small.T, tr, group_tr)

        p_a = _mm_t(h, w_in_t, l, 0, OFF_AA, r_all, name="in_proj_a")
        p_b = _mm_t(h, w_in_t, l, OFF_B, 5 * B_W, r_all, name="in_proj_b")
        p_c = _mm_t(h, w_in_t, l, OFF_C, 4 * C_W, r_all, name="in_proj_c")

        w2pad, b2pad = _alpha_weights(w_alpha2[l], b_alpha[l])
        qr, kr, bcum = _gla_prep(p_a, small, w2pad, b2pad, e_mat, pe_mat, cos_t, sin_t, tr, pos_tile)
        y_f = _gla_scan(qr, kr, p_a, bcum, bsz, seq, ctx_len, rev=False)
        y_b = _gla_scan(qr, kr, p_a, bcum, bsz, seq, ctx_len, rev=True)

        qk_b, kt_b = _ml_prep(p_b, conv_w[l], conv_b[l], tr, edge_flags)
        g = small[:, 2 * GLA_RANK:2 * GLA_RANK + 4 * ML_HEADS].reshape(r_all, 4, ML_HEADS) + b_gates[l]
        g = jnp.transpose(g, (1, 2, 0))
        h_b = _ml_scan(qk_b, kt_b, p_b, g[0::2], g[1::2], bsz, seq, ctx_len)
        ya, yb = _post(y_f, y_b, p_a, gla_norm_g[l], h_b, p_b, tr)

        yc = _na(p_c, _na_bias_table(rpb[l]), bsz, seq, ctx_len, with_ctx=not last_layer)

        n_rows = n_lat_rows if last_layer else r_all
        gates = _mm_gate(h, w_merge, b_merge.reshape(depth, 1, -1), l, n_rows, name="merge_gates")
        u = _proj_merge(ya, yb, yc, w_proj_a, w_proj_b, w_proj_c, l, gates, n_rows)
        xall = _out_proj(u, w_out, l, xall, gate, n_rows, bsz, seq)

    out = _final_norm(xall, final_g, n_lat_rows, tr)
    return out.reshape(bsz, seq, d)
```

```python
import functools

import numpy as np
import jax
import jax.numpy as jnp
from jax import lax
from jax.experimental import pallas as pl
from jax.experimental.pallas import tpu as pltpu

F32 = jnp.float32
BF16 = jnp.bfloat16
HI = lax.Precision.HIGHEST

GRID_W = 64
EPS = 1e-6
ROPE_BASE = 10000.0
GLA_HEADS, GLA_DK, GLA_DV, GLA_RANK, GLA_TAU = 8, 64, 128, 16, 16.0
ML_HEADS, ML_DH = 8, 128
NA_HEADS, NA_DH, NA_KH, NA_KW = 16, 128, 8, 16
A_QK = GLA_HEADS * GLA_DK
A_V = GLA_HEADS * GLA_DV
B_W = ML_HEADS * ML_DH
C_W = NA_HEADS * NA_DH
LANE = 128
GLA_PAD = GLA_HEADS * LANE
GLA_CHUNK = 64
GLA_SUB = 16
ML_CHUNK = 128
NA_ROWS = 8
NEG = -1e30
OFF_AA = 2 * A_QK + 2 * A_V
OFF_B = OFF_AA + 2 * GLA_RANK
OFF_GATES = OFF_B + 5 * B_W
OFF_C = OFF_GATES + 4 * ML_HEADS
VMEM_LIMIT = 52 * 1024 * 1024


def _cp(*sem):
    return pltpu.CompilerParams(dimension_semantics=sem, vmem_limit_bytes=VMEM_LIMIT)


def _nt(a, b):
    return lax.dot_general(a, b, (((1,), (1,)), ((), ())), preferred_element_type=F32)


def _tn(a, b):
    return lax.dot_general(a, b, (((0,), (0,)), ((), ())), preferred_element_type=F32)


def _dot_split(a, b):
    a_hi = a.astype(BF16)
    a_lo = (a - a_hi.astype(F32)).astype(BF16)
    b_hi = b.astype(BF16)
    b_lo = (b - b_hi.astype(F32)).astype(BF16)
    d = lambda x, y: jnp.dot(x, y, preferred_element_type=F32)
    return d(a_hi, b_hi) + (d(a_hi, b_lo) + d(a_lo, b_hi))


def _log_sigmoid(x):
    return jnp.minimum(x, 0.0) - jnp.log1p(jnp.exp(-jnp.abs(x)))


def _silu(x):
    return x * jax.nn.sigmoid(x)


def _pick(n, cands):
    for c in cands:
        if n % c == 0:
            return c
    raise ValueError(f"no tile for {n}")


def _mod_kernel(s_ref, w_ref, b_ref, o_ref, *, kc):
    s = _silu(s_ref[...]).astype(BF16)
    d = s.shape[1]
    acc = jnp.zeros(o_ref.shape[1:], F32)
    for k0 in range(0, d, kc):
        acc = acc + jnp.dot(s[:, k0:k0 + kc], w_ref[0, k0:k0 + kc, :].astype(BF16),
                            preferred_element_type=F32)
    o_ref[0] = acc + b_ref[0]


def _modulation(cc, w_mod, b_mod):
    depth, d, n = w_mod.shape
    tn = _pick(n, (512, 256, 128))
    kc = _pick(d, (512, 256, 128))
    return pl.pallas_call(
        functools.partial(_mod_kernel, kc=kc),
        grid=(depth, n // tn),
        in_specs=[pl.BlockSpec((8, d), lambda l, j: (0, 0)),
                  pl.BlockSpec((1, d, tn), lambda l, j: (l, 0, j)),
                  pl.BlockSpec((1, 1, tn), lambda l, j: (l, 0, j))],
        out_specs=pl.BlockSpec((1, 8, tn), lambda l, j: (l, 0, j)),
        out_shape=jax.ShapeDtypeStruct((depth, 8, n), F32),
        compiler_params=_cp("arbitrary", "arbitrary"),
        name="modulation",
    )(cc, w_mod, b_mod.reshape(depth, 1, n))


def _norm_kernel(x_ref, g_ref, sh_ref, sc_ref, ws_ref, h_ref, sm_ref):
    x = x_ref[...]
    xn = x * lax.rsqrt(jnp.mean(x * x, axis=1, keepdims=True) + EPS)
    h = xn * g_ref[...] * (1.0 + sc_ref[0]) + sh_ref[0]
    h_ref[...] = h.astype(BF16)
    sm_ref[...] = _dot_split(h, ws_ref[...])


def _norm_mod(xall, g, shift, scale, w_small, tr, group_of):
    r, d = xall.shape
    return pl.pallas_call(
        _norm_kernel,
        grid=(r // tr,),
        in_specs=[pl.BlockSpec((tr, d), lambda i: (i, 0)),
                  pl.BlockSpec((1, d), lambda i: (0, 0)),
                  pl.BlockSpec((1, 1, d), lambda i: (group_of(i), 0, 0)),
                  pl.BlockSpec((1, 1, d), lambda i: (group_of(i), 0, 0)),
                  pl.BlockSpec((d, LANE), lambda i: (0, 0))],
        out_specs=[pl.BlockSpec((tr, d), lambda i: (i, 0)),
                   pl.BlockSpec((tr, LANE), lambda i: (i, 0))],
        out_shape=[jax.ShapeDtypeStruct((r, d), BF16),
                   jax.ShapeDtypeStruct((r, LANE), F32)],
        compiler_params=_cp("arbitrary"),
        name="norm_mod",
    )(xall, g.reshape(1, d), shift, scale, w_small)


def _final_norm_kernel(x_ref, g_ref, o_ref):
    x = x_ref[...]
    o_ref[...] = x * lax.rsqrt(jnp.mean(x * x, axis=1, keepdims=True) + EPS) * g_ref[...]


def _final_norm(x, g, n_rows, tr):
    d = x.shape[1]
    return pl.pallas_call(
        _final_norm_kernel,
        grid=(n_rows // tr,),
        in_specs=[pl.BlockSpec((tr, d), lambda i: (i, 0)),
                  pl.BlockSpec((1, d), lambda i: (0, 0))],
        out_specs=pl.BlockSpec((tr, d), lambda i: (i, 0)),
        out_shape=jax.ShapeDtypeStruct((n_rows, d), F32),
        compiler_params=_cp("arbitrary"),
        name="final_norm",
    )(x, g.reshape(1, d))


def _row_tile(n_rows):
    for t in range(1152, 127, -16):
        if n_rows % t == 0:
            return t
    raise ValueError(f"no row tile for {n_rows}")


def _cast_weight(w_ref, wb_ref):
    k = wb_ref.shape[0]
    kc = _pick(k, (512, 256, 128))
    for k0 in range(0, k, kc):
        wb_ref[k0:k0 + kc, :] = w_ref[0, k0:k0 + kc, :].astype(BF16)


def _mm_gate_kernel(x_ref, w_ref, b_ref, o_ref, wb_ref):
    @pl.when(pl.program_id(1) == 0)
    def _():
        _cast_weight(w_ref, wb_ref)

    acc = jnp.dot(x_ref[...], wb_ref[...], preferred_element_type=F32)
    o_ref[...] = jax.nn.sigmoid(acc + b_ref[0]).astype(o_ref.dtype)


def _mm_gate(x, w, bias, layer, n_rows, name):
    k, n = w.shape[1:]
    tn = _pick(n, (512, 256, 128))
    tm = _row_tile(n_rows)
    return pl.pallas_call(
        _mm_gate_kernel,
        grid=(n // tn, n_rows // tm),
        in_specs=[pl.BlockSpec((tm, k), lambda j, i: (i, 0)),
                  pl.BlockSpec((1, k, tn), lambda j, i: (layer, 0, j)),
                  pl.BlockSpec((1, 1, tn), lambda j, i: (layer, 0, j))],
        out_specs=pl.BlockSpec((tm, tn), lambda j, i: (i, j)),
        out_shape=jax.ShapeDtypeStruct((n_rows, n), BF16),
        scratch_shapes=[pltpu.VMEM((k, tn), BF16)],
        compiler_params=_cp("arbitrary", "arbitrary"),
        name=name,
    )(x, w, bias)


def _mm_t_kernel(x_ref, w_ref, o_ref, wb_ref):
    @pl.when(pl.program_id(1) == 0)
    def _():
        for r0 in range(0, wb_ref.shape[0], LANE):
            wb_ref[r0:r0 + LANE, :] = w_ref[0, r0:r0 + LANE, :].astype(BF16)

    o_ref[...] = _nt(x_ref[...], wb_ref[...]).astype(o_ref.dtype)


def _mm_t(x, w_t, layer, row0, n, n_rows, name):
    k = w_t.shape[2]
    tn = _pick(n, (512, 256, 128))
    tm = _row_tile(n_rows)
    assert row0 % 8 == 0
    if row0 % tn == 0:
        wspec = pl.BlockSpec((1, tn, k), lambda j, i: (layer, row0 // tn + j, 0))
    else:
        wspec = pl.BlockSpec((pl.Element(1), pl.Element(tn), pl.Element(k)),
                             lambda j, i: (layer, pl.multiple_of(row0 + j * tn, 8), 0))
    return pl.pallas_call(
        _mm_t_kernel,
        grid=(n // tn, n_rows // tm),
        in_specs=[pl.BlockSpec((tm, k), lambda j, i: (i, 0)), wspec],
        out_specs=pl.BlockSpec((tm, tn), lambda j, i: (i, j)),
        out_shape=jax.ShapeDtypeStruct((n_rows, n), BF16),
        scratch_shapes=[pltpu.VMEM((tn, k), BF16)],
        compiler_params=_cp("arbitrary", "arbitrary"),
        name=name,
    )(x, w_t)


def _proj_kernel(ya_ref, yb_ref, yc_ref, wa_ref, wb_ref, wc_ref, ga_ref, gb_ref, gc_ref,
                 o_ref, wa_s, wb_s, wc_s):
    @pl.when(pl.program_id(1) == 0)
    def _():
        _cast_weight(wa_ref, wa_s)
        _cast_weight(wb_ref, wb_s)
        _cast_weight(wc_ref, wc_s)

    u = ga_ref[...].astype(F32) * jnp.dot(ya_ref[...], wa_s[...], preferred_element_type=F32)
    u = u + gb_ref[...].astype(F32) * jnp.dot(yb_ref[...], wb_s[...], preferred_element_type=F32)
    u = u + gc_ref[...].astype(F32) * jnp.dot(yc_ref[...], wc_s[...], preferred_element_type=F32)
    o_ref[...] = u.astype(o_ref.dtype)


def _proj_merge(ya, yb, yc, wa, wb, wc, layer, gates, n_rows):
    d = wa.shape[2]
    tn = _pick(d, (512, 256, 128))
    tm = _row_tile(n_rows)
    nj = d // tn
    row = lambda j, i: (i, 0)
    col = lambda j, i: (layer, 0, j)
    return pl.pallas_call(
        _proj_kernel,
        grid=(nj, n_rows // tm),
        in_specs=[pl.BlockSpec((tm, ya.shape[1]), row),
                  pl.BlockSpec((tm, yb.shape[1]), row),
                  pl.BlockSpec((tm, yc.shape[1]), row),
                  pl.BlockSpec((1, wa.shape[1], tn), col),
                  pl.BlockSpec((1, wb.shape[1], tn), col),
                  pl.BlockSpec((1, wc.shape[1], tn), col),
                  pl.BlockSpec((tm, tn), lambda j, i: (i, j)),
                  pl.BlockSpec((tm, tn), lambda j, i: (i, nj + j)),
                  pl.BlockSpec((tm, tn), lambda j, i: (i, 2 * nj + j))],
        out_specs=pl.BlockSpec((tm, tn), lambda j, i: (i, j)),
        out_shape=jax.ShapeDtypeStruct((n_rows, d), BF16),
        scratch_shapes=[pltpu.VMEM((wa.shape[1], tn), BF16),
                        pltpu.VMEM((wb.shape[1], tn), BF16),
                        pltpu.VMEM((wc.shape[1], tn), BF16)],
        compiler_params=_cp("arbitrary", "arbitrary"),
        name="proj_merge",
    )(ya, yb, yc, wa, wb, wc, gates, gates, gates)


def _out_kernel(u_ref, w_ref, x_ref, g_ref, o_ref, wb_ref, *, bsz, seq):
    @pl.when(pl.program_id(1) == 0)
    def _():
        _cast_weight(w_ref, wb_ref)

    acc = jnp.dot(u_ref[...], wb_ref[...], preferred_element_type=F32)
    tm = acc.shape[0]
    row = pl.program_id(1) * tm + lax.broadcasted_iota(jnp.int32, (tm, 1), 0)
    g = g_ref[bsz:bsz + 1, :]
    for b in reversed(range(bsz)):
        g = jnp.where(row < (b + 1) * seq, g_ref[b:b + 1, :], g)
    o_ref[...] = x_ref[...] + g * acc


def _out_proj(u, w, layer, xall, gate, n_rows, bsz, seq):
    k, n = w.shape[1:]
    tn = _pick(n, (512, 256, 128))
    tm = _row_tile(n_rows)
    return pl.pallas_call(
        functools.partial(_out_kernel, bsz=bsz, seq=seq),
        grid=(n // tn, n_rows // tm),
        in_specs=[pl.BlockSpec((tm, k), lambda j, i: (i, 0)),
                  pl.BlockSpec((1, k, tn), lambda j, i: (layer, 0, j)),
                  pl.BlockSpec((tm, tn), lambda j, i: (i, j)),
                  pl.BlockSpec((8, tn), lambda j, i: (0, j))],
        out_specs=pl.BlockSpec((tm, tn), lambda j, i: (i, j)),
        out_shape=jax.ShapeDtypeStruct((n_rows, n), F32),
        scratch_shapes=[pltpu.VMEM((k, tn), BF16)],
        compiler_params=_cp("arbitrary", "arbitrary"),
        name="out_proj",
    )(u, w, xall, gate)


def _gla_prep_kernel(qk_ref, sm_ref, w2_ref, b2_ref, e_ref, pe_ref, cos_ref, sin_ref, tri_ref,
                     q_out, k_out, b_out):
    z = _dot_split(sm_ref[...], w2_ref[...]) + b2_ref[...]
    la = _log_sigmoid(z) * (1.0 / GLA_TAU)
    for d in range(2):
        rest = la[:, d * GLA_PAD:(d + 1) * GLA_PAD]
        acc = jnp.zeros(rest.shape, F32)
        for _ in range(3):
            part = rest.astype(BF16)
            acc = acc + jnp.dot(tri_ref[d], part, preferred_element_type=F32)
            rest = rest - part.astype(F32)
        b_out[d] = acc
    cos = cos_ref[...]
    sin = sin_ref[...]
    e = e_ref[...]
    pe = pe_ref[...]
    q = qk_ref[:, :A_QK]
    k = qk_ref[:, A_QK:]
    rope = lambda t: (jnp.dot(t, e, preferred_element_type=F32) * cos
                      + jnp.dot(t, pe, preferred_element_type=F32) * sin)
    q_out[...] = rope(q) * (GLA_DK ** -0.5)
    k_out[...] = rope(k)


def _gla_prep(p_a, small, w2pad, b2pad, e_mat, pe_mat, cos_t, sin_t, tr, pos_tile):
    r = p_a.shape[0]
    full = lambda i: (0, 0)
    idx = np.arange(tr)
    same_chunk = (idx[None, :] // GLA_CHUNK) == (idx[:, None] // GLA_CHUNK)
    tri = np.stack([same_chunk & (idx[None, :] <= idx[:, None]),
                    same_chunk & (idx[None, :] >= idx[:, None])]).astype(np.float32)
    tri = jnp.asarray(tri, BF16)
    return pl.pallas_call(
        _gla_prep_kernel,
        grid=(r // tr,),
        in_specs=[pl.BlockSpec((tr, 2 * A_QK), lambda i: (i, 0)),
                  pl.BlockSpec((tr, LANE), lambda i: (i, 0)),
                  pl.BlockSpec(w2pad.shape, full),
                  pl.BlockSpec(b2pad.shape, full),
                  pl.BlockSpec(e_mat.shape, full),
                  pl.BlockSpec(pe_mat.shape, full),
                  pl.BlockSpec((tr, GLA_PAD), lambda i: (pos_tile(i), 0)),
                  pl.BlockSpec((tr, GLA_PAD), lambda i: (pos_tile(i), 0)),
                  pl.BlockSpec(tri.shape, lambda i: (0, 0, 0))],
        out_specs=[pl.BlockSpec((tr, GLA_PAD), lambda i: (i, 0)),
                   pl.BlockSpec((tr, GLA_PAD), lambda i: (i, 0)),
                   pl.BlockSpec((2, tr, GLA_PAD), lambda i: (0, i, 0))],
        out_shape=[jax.ShapeDtypeStruct((r, GLA_PAD), F32),
                   jax.ShapeDtypeStruct((r, GLA_PAD), F32),
                   jax.ShapeDtypeStruct((2, r, GLA_PAD), F32)],
        compiler_params=_cp("arbitrary"),
        name="gla_prep",
    )(p_a, small, w2pad, b2pad, e_mat, pe_mat, cos_t, sin_t, tri)


def _gla_masks(rev):
    c = GLA_CHUNK
    idx = np.arange(c)
    pos = c - 1 - idx if rev else idx
    rb = pos // GLA_SUB
    rh = pos // (2 * GLA_SUB)
    md = (pos[None, :] <= pos[:, None]) & (rb[None, :] == rb[:, None])
    m16 = (rh[None, :] == rh[:, None]) & (rb[None, :] < rb[:, None])
    return np.stack([md, m16]).astype(np.float32)


def _gla_scan_kernel(qf_ref, kf_ref, vf_ref, bf_ref, qb_ref, kb_ref, vb_ref, bb_ref, mask_ref,
                     yf_ref, yb_ref, st_ref, dec_s, att_s, yi_s, *ops_s):
    sub = GLA_SUB
    nb = GLA_CHUNK // sub
    assert nb == 4

    @pl.when(pl.program_id(1) == 0)
    def _():
        st_ref[...] = jnp.zeros(st_ref.shape, F32)

    rows = lambda kb: slice(kb * sub, (kb + 1) * sub)
    cat = lambda parts: jnp.concatenate(parts, axis=0)
    zeros = jnp.zeros((sub, LANE), BF16)
    heads = [slice(h * LANE, (h + 1) * LANE) for h in range(GLA_HEADS)]
    qd_s, kd_s, k16_s, q32_s, k32_s, qi_s, ka_s = ops_s
    streams = [(0, False, qf_ref, kf_ref, vf_ref, bf_ref, yf_ref),
               (1, True, qb_ref, kb_ref, vb_ref, bb_ref, yb_ref)]
    for d, rev, q_ref, k_ref, v_ref, b_ref, y_ref in streams:
        rank = lambda kb: nb - 1 - kb if rev else kb
        by_rank = lambda rk: nb - 1 - rk if rev else rk
        for h, sl in enumerate(heads):
            b = b_ref[0, :, sl]
            q = q_ref[:, sl]
            k = k_ref[:, sl]

            def end(rk):
                kb = by_rank(rk)
                r0 = kb * sub if rev else kb * sub + sub - 1
                return b[r0:r0 + 1]

            tot = end(nb - 1)
            dq = cat([b[rows(kb)] - end(rank(kb) - 1) if rank(kb) > 0 else b[rows(kb)]
                      for kb in range(nb)])
            qd_s[d, h] = (q * jnp.exp(dq)).astype(BF16)
            kd_s[d, h] = (k * jnp.exp(-dq)).astype(BF16)
            k16_s[d, h] = cat([(k[rows(kb)] * jnp.exp(end(rank(kb)) - b[rows(kb)])).astype(BF16)
                               if rank(kb) % 2 == 0 else zeros for kb in range(nb)])
            half_end = end(nb // 2 - 1)
            q32_s[d, h] = cat([(q[rows(kb)] * jnp.exp(b[rows(kb)] - half_end)).astype(BF16)
                               if rank(kb) >= nb // 2 else zeros for kb in range(nb)])
            k32_s[d, h] = cat([(k[rows(kb)] * jnp.exp(half_end - b[rows(kb)])).astype(BF16)
                               if rank(kb) < nb // 2 else zeros for kb in range(nb)])
            qi_s[d, h] = (q * jnp.exp(b)).astype(BF16)
            ka_s[d, h] = (k * jnp.exp(tot - b)).astype(BF16)
            dec_s[d, h] = jnp.exp(tot)
    for d, rev, q_ref, k_ref, v_ref, b_ref, y_ref in streams:
        md = mask_ref[d, 0] > 0.0
        m16 = mask_ref[d, 1]
        for h, sl in enumerate(heads):
            qd = qd_s[d, h]
            att = (jnp.where(md, _nt(qd, kd_s[d, h]), 0.0) + m16 * _nt(qd, k16_s[d, h])
                   + _nt(q32_s[d, h], k32_s[d, h]))
            att_s[d, h] = att.astype(BF16)
            yi_s[d, h] = _nt(qi_s[d, h], st_ref[d, h].astype(BF16))
    for d, rev, q_ref, k_ref, v_ref, b_ref, y_ref in streams:
        for h, sl in enumerate(heads):
            v = v_ref[:, sl]
            y_ref[:, sl] = (jnp.dot(att_s[d, h], v, preferred_element_type=F32)
                            + yi_s[d, h]).astype(y_ref.dtype)
            st_ref[d, h] = st_ref[d, h] * dec_s[d, h] + _tn(v, ka_s[d, h])


def _chunk_index(bsz, n_lat, n_ctx, lat0, ctx0):
    def f(b, d, c):
        in_ctx = c < n_ctx
        cc = jnp.where(d == 0, c, n_ctx - 1 - c)
        cl = jnp.where(d == 0, c - n_ctx, n_lat - 1 - (c - n_ctx))
        return jnp.where(in_ctx, ctx0 + b * n_ctx + cc, lat0 + b * n_lat + cl)
    return f


def _gla_scan(qr, kr, p_a, bcum, bsz, seq, ctx_len):
    r = qr.shape[0]
    c = GLA_CHUNK
    masks = jnp.asarray(np.stack([_gla_masks(False), _gla_masks(True)]))
    n_lat, n_ctx = seq // c, ctx_len // c
    blk = _chunk_index(bsz, n_lat, n_ctx, 0, bsz * n_lat)
    v_col = (2 * A_QK) // A_V
    nh = GLA_HEADS

    def stream(d):
        return [pl.BlockSpec((c, GLA_PAD), lambda b, i: (blk(b, d, i), 0)),
                pl.BlockSpec((c, GLA_PAD), lambda b, i: (blk(b, d, i), 0)),
                pl.BlockSpec((c, A_V), lambda b, i: (blk(b, d, i), v_col)),
                pl.BlockSpec((1, c, GLA_PAD), lambda b, i: (d, blk(b, d, i), 0))]

    out = lambda d: pl.BlockSpec((c, A_V), lambda b, i: (blk(b, d, i), 0))
    return pl.pallas_call(
        _gla_scan_kernel,
        grid=(bsz, n_lat + n_ctx),
        in_specs=stream(0) + stream(1) + [pl.BlockSpec(masks.shape, lambda b, i: (0, 0, 0, 0))],
        out_specs=[out(0), out(1)],
        out_shape=[jax.ShapeDtypeStruct((r, A_V), BF16)] * 2,
        scratch_shapes=[pltpu.VMEM((2, nh, GLA_DV, LANE), F32),
                        pltpu.VMEM((2, nh, 1, LANE), F32),
                        pltpu.VMEM((2, nh, c, c), BF16),
                        pltpu.VMEM((2, nh, c, GLA_DV), F32)]
        + [pltpu.VMEM((2, nh, c, LANE), BF16)] * 7,
        compiler_params=_cp("arbitrary", "arbitrary"),
        name="gla_scan",
    )(qr, kr, p_a, bcum, qr, kr, p_a, bcum, masks)


def _ml_prep_kernel(x_ref, xp_ref, xn_ref, w_ref, b_ref, first_ref, o_ref, kt_ref):
    i = pl.program_id(0)
    x = x_ref[...].astype(F32)
    tr = x.shape[0]
    is_first = first_ref[i, 0] > 0
    is_last = first_ref[i, 1] > 0
    prev = jnp.where(is_first, 0.0, xp_ref[7:8, :].astype(F32))
    nxt = jnp.where(is_last, 0.0, xn_ref[0:1, :].astype(F32))
    row = lax.broadcasted_iota(jnp.int32, x.shape, 0)
    xm1 = jnp.where(row == 0, prev, pltpu.roll(x, 1, axis=0))
    xp1 = jnp.where(row == tr - 1, nxt, pltpu.roll(x, tr - 1, axis=0))
    y = xm1 * w_ref[0:1, :] + x * w_ref[1:2, :] + xp1 * w_ref[2:3, :] + b_ref[...]
    y = _silu(y)
    k = y[:, B_W:] * (ML_DH ** -0.5)
    o_ref[:, :B_W] = y[:, :B_W].astype(BF16)
    o_ref[:, B_W:] = k.astype(BF16)
    kt_ref[...] = k.T.astype(BF16)


def _ml_prep(p_b, conv_w, conv_b, tr, edge_flags):
    r = p_b.shape[0]
    w = 2 * B_W
    nb = r // 8
    return pl.pallas_call(
        _ml_prep_kernel,
        grid=(r // tr,),
        in_specs=[pl.BlockSpec((tr, w), lambda i: (i, 0)),
                  pl.BlockSpec((8, w), lambda i: (jnp.maximum(i * (tr // 8) - 1, 0), 0)),
                  pl.BlockSpec((8, w), lambda i: (jnp.minimum((i + 1) * (tr // 8), nb - 1), 0)),
                  pl.BlockSpec((8, w), lambda i: (0, 0)),
                  pl.BlockSpec((1, w), lambda i: (0, 0)),
                  pl.BlockSpec(memory_space=pltpu.SMEM)],
        out_specs=[pl.BlockSpec((tr, w), lambda i: (i, 0)),
                   pl.BlockSpec((B_W, tr), lambda i: (0, i))],
        out_shape=[jax.ShapeDtypeStruct((r, w), BF16),
                   jax.ShapeDtypeStruct((B_W, r), BF16)],
        compiler_params=_cp("arbitrary"),
        name="mlstm_prep",
    )(p_b, p_b, p_b, jnp.pad(conv_w, ((0, 8 - conv_w.shape[0]), (0, 0))), conv_b.reshape(1, w),
      edge_flags)


def _ml_scan_kernel(qkf_ref, ktf_ref, vf_ref, gif_ref, gff_ref, qkb_ref, ktb_ref, vb_ref, gib_ref,
                    gfb_ref, tri_ref, yf_ref, yb_ref, s_ref, m_ref, qk_s, qs_s, pk_s, wi_s, kw_s):
    c = ML_CHUNK
    nh = ML_HEADS
    assert c == ML_DH

    @pl.when(pl.program_id(1) == 0)
    def _():
        s_ref[...] = jnp.zeros(s_ref.shape, F32)
        m_ref[...] = jnp.zeros(m_ref.shape, F32)

    lane = lax.broadcasted_iota(jnp.int32, (nh, c), 1)
    ones = jnp.ones((c, ML_DH), BF16)
    heads = [slice(h * ML_DH, (h + 1) * ML_DH) for h in range(nh)]
    streams = [(0, qkf_ref, ktf_ref, vf_ref, gif_ref, gff_ref, yf_ref),
               (1, qkb_ref, ktb_ref, vb_ref, gib_ref, gfb_ref, yb_ref)]
    gates = []
    for d, qk_ref, kt_ref, v_ref, git_ref, gft_ref, y_ref in streams:
        tri = tri_ref[d]
        lft = _log_sigmoid(gft_ref[0])
        brow = lax.dot_general(lft, tri, (((1,), (1,)), ((), ())), precision=HI,
                               preferred_element_type=F32)
        tot = jnp.sum(lft, axis=1, keepdims=True)
        g = git_ref[0] - brow
        run = g
        sh = 1
        while sh < c:
            if d == 0:
                run = jnp.maximum(run, jnp.where(lane >= sh, pltpu.roll(run, sh, axis=1), NEG))
            else:
                run = jnp.maximum(run, jnp.where(lane < c - sh, pltpu.roll(run, c - sh, axis=1), NEG))
            sh *= 2
        m_prev = m_ref[d]
        big_m = jnp.maximum(m_prev, run)
        m_next = jnp.maximum(m_prev, jnp.max(g, axis=1, keepdims=True))
        decay = jnp.exp(m_prev - m_next)
        w_rows = jnp.exp(g - m_next)
        cols = jnp.concatenate([big_m, brow + big_m, jnp.zeros((c - 2 * nh, c), F32)], axis=0).T
        gates.append((g, m_prev, decay, w_rows, cols))
        m_ref[d] = tot + m_next
    for d, qk_ref, kt_ref, v_ref, git_ref, gft_ref, y_ref in streams:
        for h, sl in enumerate(heads):
            q = qk_ref[:, sl]
            qk_s[d, h] = jnp.dot(q, kt_ref[sl, :], preferred_element_type=F32)
            qs_s[d, h] = jnp.dot(q, s_ref[d, h].astype(BF16), preferred_element_type=F32)
    for d, qk_ref, kt_ref, v_ref, git_ref, gft_ref, y_ref in streams:
        g, m_prev, decay, w_rows, cols = gates[d]
        causal = tri_ref[d] > 0.0
        for h, sl in enumerate(heads):
            m_mat = jnp.broadcast_to(cols[:, h:h + 1], (c, c))
            p = jnp.where(causal, jnp.exp(g[h:h + 1, :] - m_mat), 0.0)
            pk_s[d, h] = (qk_s[d, h] * p).astype(BF16)
            wi_s[d, h] = jnp.exp(m_prev[h:h + 1, :] - m_mat)
            kw_s[d, h] = (kt_ref[sl, :].astype(F32) * w_rows[h:h + 1, :]).astype(BF16)
    for d, qk_ref, kt_ref, v_ref, git_ref, gft_ref, y_ref in streams:
        g, m_prev, decay, w_rows, cols = gates[d]
        for h, sl in enumerate(heads):
            vo = jnp.concatenate([v_ref[:, sl], ones], axis=1)
            w_inter = wi_s[d, h]
            nd = (jnp.dot(pk_s[d, h], vo, preferred_element_type=F32)
                  + jnp.concatenate([w_inter, w_inter], axis=1) * qs_s[d, h])
            neg_m = jnp.exp(-jnp.broadcast_to(cols[:, nh + h:nh + h + 1], (c, ML_DH)))
            y_ref[:, sl] = (nd[:, :ML_DH] / jnp.maximum(jnp.abs(nd[:, ML_DH:]), neg_m)).astype(y_ref.dtype)
            s_ref[d, h] = (decay[h:h + 1, :] * s_ref[d, h]
                           + jnp.dot(kw_s[d, h], vo, preferred_element_type=F32))


def _ml_scan(qk, kt, p_b, git, gft, bsz, seq, ctx_len):
    r = qk.shape[0]
    c = ML_CHUNK
    idx = np.arange(c)
    tri = np.stack([(idx[None, :] <= idx[:, None]), (idx[None, :] >= idx[:, None])]).astype(np.float32)
    n_lat, n_ctx = seq // c, ctx_len // c
    blk = _chunk_index(bsz, n_lat, n_ctx, 0, bsz * n_lat)
    h = ML_HEADS

    def stream(d):
        return [pl.BlockSpec((c, 2 * B_W), lambda b, i: (blk(b, d, i), 0)),
                pl.BlockSpec((B_W, c), lambda b, i: (0, blk(b, d, i))),
                pl.BlockSpec((c, B_W), lambda b, i: (blk(b, d, i), 2)),
                pl.BlockSpec((1, h, c), lambda b, i: (d, 0, blk(b, d, i))),
                pl.BlockSpec((1, h, c), lambda b, i: (d, 0, blk(b, d, i)))]

    out = lambda d: pl.BlockSpec((c, B_W), lambda b, i: (blk(b, d, i), 0))
    return pl.pallas_call(
        _ml_scan_kernel,
        grid=(bsz, n_lat + n_ctx),
        in_specs=stream(0) + stream(1) + [pl.BlockSpec((2, c, c), lambda b, i: (0, 0, 0))],
        out_specs=[out(0), out(1)],
        out_shape=[jax.ShapeDtypeStruct((r, B_W), BF16)] * 2,
        scratch_shapes=[pltpu.VMEM((2, h, ML_DH, 2 * ML_DH), F32),
                        pltpu.VMEM((2, h, 1), F32),
                        pltpu.VMEM((2, h, c, c), F32),
                        pltpu.VMEM((2, h, c, 2 * ML_DH), F32),
                        pltpu.VMEM((2, h, c, c), BF16),
                        pltpu.VMEM((2, h, c, c), F32),
                        pltpu.VMEM((2, h, ML_DH, c), BF16)],
        compiler_params=_cp("arbitrary", "arbitrary"),
        name="mlstm_scan",
    )(qk, kt, p_b, git, gft, qk, kt, p_b, git, gft, jnp.asarray(tri))


def _post_kernel(yf_ref, yb_ref, za_ref, g_ref, hf_ref, hb_ref, zb_ref, ob_ref, oa_ref, obb_ref):
    for h in range(GLA_HEADS):
        sl = slice(h * GLA_DV, (h + 1) * GLA_DV)
        o = yf_ref[:, sl].astype(F32) + yb_ref[:, sl].astype(F32)
        o = o * lax.rsqrt(jnp.mean(o * o, axis=1, keepdims=True) + EPS) * g_ref[:, sl]
        oa_ref[:, sl] = (o * _silu(za_ref[:, sl].astype(F32))).astype(BF16)
    hb = hf_ref[...].astype(F32) + hb_ref[...].astype(F32)
    obb_ref[...] = (jax.nn.sigmoid(ob_ref[...].astype(F32)) * hb
                    * _silu(zb_ref[...].astype(F32))).astype(BF16)


def _post(y_f, y_b, p_a, gnorm, h_f, h_b, p_b, tr):
    r = p_a.shape[0]
    za_col = (2 * A_QK + A_V) // A_V
    return pl.pallas_call(
        _post_kernel,
        grid=(r // tr,),
        in_specs=[pl.BlockSpec((tr, A_V), lambda i: (i, 0)),
                  pl.BlockSpec((tr, A_V), lambda i: (i, 0)),
                  pl.BlockSpec((tr, A_V), lambda i: (i, za_col)),
                  pl.BlockSpec((1, A_V), lambda i: (0, 0)),
                  pl.BlockSpec((tr, B_W), lambda i: (i, 0)),
                  pl.BlockSpec((tr, B_W), lambda i: (i, 0)),
                  pl.BlockSpec((tr, B_W), lambda i: (i, 3)),
                  pl.BlockSpec((tr, B_W), lambda i: (i, 4))],
        out_specs=[pl.BlockSpec((tr, A_V), lambda i: (i, 0)),
                   pl.BlockSpec((tr, B_W), lambda i: (i, 0))],
        out_shape=[jax.ShapeDtypeStruct((r, A_V), BF16),
                   jax.ShapeDtypeStruct((r, B_W), BF16)],
        compiler_params=_cp("arbitrary"),
        name="post_ab",
    )(y_f, y_b, p_a, gnorm.reshape(1, A_V), h_f, h_b, p_b, p_b)


def _na_kernel(q_ref, k_ref, v_ref, z_ref, qc_ref, kc_ref, vc_ref, zc_ref, t_ref, o_ref,
               s_ref, p_ref, *, rows, bsz, ctx_len, with_ctx):
    b = pl.program_id(0)
    scale = NA_DH ** -0.5
    if with_ctx:
        @pl.when(b == bsz)
        def _():
            for bb in range(bsz):
                rs_ = slice(bb * ctx_len, (bb + 1) * ctx_len)
                q = (qc_ref[rs_, :].astype(F32) * scale).astype(BF16)
                s = _nt(q, kc_ref[rs_, :])
                p = jnp.exp(s - jnp.max(s, axis=1, keepdims=True))
                o = (jnp.dot(p.astype(BF16), vc_ref[rs_, :], preferred_element_type=F32)
                     / jnp.sum(p, axis=1, keepdims=True))
                o_ref[rs_, :] = (o * _silu(zc_ref[rs_, :].astype(F32))).astype(BF16)
            o_ref[bsz * ctx_len:, :] = jnp.zeros((o_ref.shape[0] - bsz * ctx_len, NA_DH), BF16)

    @pl.when(b < bsz)
    def _():
        _na_latent(q_ref, k_ref, v_ref, z_ref, kc_ref, vc_ref, t_ref, o_ref, s_ref, p_ref,
                   rows=rows, c0=pl.multiple_of(b * ctx_len, ctx_len), ctx_len=ctx_len)


def _na_latent(q_ref, k_ref, v_ref, z_ref, kc_ref, vc_ref, t_ref, o_ref, s_ref, p_ref, *,
               rows, c0, ctx_len):
    kc = kc_ref[pl.ds(c0, ctx_len), :]
    vc = vc_ref[pl.ds(c0, ctx_len), :]
    win = NA_KH * GRID_W
    nq = NA_ROWS * GRID_W

    def body(t, carry):
        q0 = pl.multiple_of(t * nq, nq)
        q = (q_ref[pl.ds(q0, nq), :].astype(F32) * (NA_DH ** -0.5)).astype(BF16)
        starts = []
        for j in range(NA_ROWS):
            r = t * NA_ROWS + j
            rs = jnp.clip(r - NA_KH // 2, 0, rows - NA_KH)
            k0 = pl.multiple_of(rs * GRID_W, GRID_W)
            starts.append(k0)
            qs = slice(j * GRID_W, (j + 1) * GRID_W)
            s_ref[qs, :] = _nt(q[qs], k_ref[pl.ds(k0, win), :]) + t_ref[0, r - rs]
        s_c = _nt(q, kc)
        s_w = s_ref[...]
        m = jnp.maximum(jnp.max(s_w, axis=1, keepdims=True), jnp.max(s_c, axis=1, keepdims=True))
        p_w = jnp.exp(s_w - m)
        p_c = jnp.exp(s_c - m)
        l = jnp.sum(p_w, axis=1, keepdims=True) + jnp.sum(p_c, axis=1, keepdims=True)
        p_ref[...] = p_w.astype(BF16)
        o_c = jnp.dot(p_c.astype(BF16), vc, preferred_element_type=F32)
        o_w = jnp.concatenate(
            [jnp.dot(p_ref[j * GRID_W:(j + 1) * GRID_W, :], v_ref[pl.ds(starts[j], win), :],
                     preferred_element_type=F32) for j in range(NA_ROWS)], axis=0)
        z = z_ref[pl.ds(q0, nq), :].astype(F32)
        o_ref[pl.ds(q0, nq), :] = ((o_w + o_c) / l * _silu(z)).astype(BF16)
        return carry

    lax.fori_loop(0, rows // NA_ROWS, body, 0)


def _na_bias_table(rpb):
    c = np.arange(GRID_W)
    cs = np.clip(c - NA_KW // 2, 0, GRID_W - NA_KW)
    valid = (c[None, :] >= cs[:, None]) & (c[None, :] < cs[:, None] + NA_KW)
    dc = c[None, :] - c[:, None] + NA_KW - 1
    ndc = 2 * NA_KW - 1
    onehot = ((dc[None] == np.arange(ndc)[:, None, None]) & valid[None]).astype(np.float32)
    full = jnp.einsum("hdj,jck->hdck", rpb, jnp.asarray(onehot), precision=HI)
    full = full + jnp.asarray(np.where(valid, 0.0, NEG).astype(np.float32))
    t = jnp.stack([full[:, NA_KH - 1 - oi:2 * NA_KH - 1 - oi] for oi in range(NA_KH)], axis=1)
    t = jnp.transpose(t, (0, 1, 3, 2, 4))
    return t.reshape(NA_HEADS, NA_KH, GRID_W, NA_KH * GRID_W)


def _na(p_c, table, bsz, seq, ctx_len, with_ctx):
    rows = seq // GRID_W
    n_ctx = bsz * ctx_len
    assert (bsz * seq) % n_ctx == 0 and n_ctx <= seq
    ctx_blk = (bsz * seq) // n_ctx
    hq, hk, hv, hz = 0, NA_HEADS, 2 * NA_HEADS, 3 * NA_HEADS
    lat = lambda off: pl.BlockSpec((seq, NA_DH), lambda b, h: (jnp.minimum(b, bsz - 1), off + h))
    ctx = lambda off: pl.BlockSpec((n_ctx, NA_DH), lambda b, h: (ctx_blk, off + h))
    nb = bsz + int(with_ctx)
    return pl.pallas_call(
        functools.partial(_na_kernel, rows=rows, bsz=bsz, ctx_len=ctx_len, with_ctx=with_ctx),
        grid=(nb, NA_HEADS),
        in_specs=[lat(hq), lat(hk), lat(hv), lat(hz), ctx(hq), ctx(hk), ctx(hv), ctx(hz),
                  pl.BlockSpec((1,) + table.shape[1:], lambda b, h: (h, 0, 0, 0))],
        out_specs=pl.BlockSpec((seq, NA_DH), lambda b, h: (b, h)),
        out_shape=jax.ShapeDtypeStruct((nb * seq, C_W), BF16),
        scratch_shapes=[pltpu.VMEM((NA_ROWS * GRID_W, NA_KH * GRID_W), F32),
                        pltpu.VMEM((NA_ROWS * GRID_W, NA_KH * GRID_W), BF16)],
        compiler_params=_cp("arbitrary", "arbitrary"),
        name="nbr_attn",
    )(p_c, p_c, p_c, p_c, p_c, p_c, p_c, p_c, table)


def _rope_tables(seq, ctx_len):
    nf = GLA_DK // 4
    t = np.arange(seq)
    inv = ROPE_BASE ** (-np.arange(nf, dtype=np.float32) / nf)
    ar = (t // GRID_W).astype(np.float32)[:, None] * inv[None, :]
    ac = (t % GRID_W).astype(np.float32)[:, None] * inv[None, :]
    ar, ac = jnp.asarray(ar), jnp.asarray(ac)
    cos = jnp.concatenate([jnp.cos(ar), jnp.cos(ar), jnp.cos(ac), jnp.cos(ac)], axis=1)
    sin = jnp.concatenate([-jnp.sin(ar), jnp.sin(ar), -jnp.sin(ac), jnp.sin(ac)], axis=1)
    cos = jnp.concatenate([cos, jnp.ones((ctx_len, GLA_DK), F32)], axis=0)
    sin = jnp.concatenate([sin, jnp.zeros((ctx_len, GLA_DK), F32)], axis=0)
    pad = jnp.zeros((seq + ctx_len, LANE - GLA_DK), F32)
    cos = jnp.tile(jnp.concatenate([cos, pad], axis=1), (1, GLA_HEADS))
    sin = jnp.tile(jnp.concatenate([sin, pad], axis=1), (1, GLA_HEADS))
    return cos, sin


def _rope_mats():
    e = np.zeros((A_QK, GLA_PAD), np.float32)
    pe = np.zeros((A_QK, GLA_PAD), np.float32)
    quarter = GLA_DK // 4
    for i in range(A_QK):
        h, j = divmod(i, GLA_DK)
        e[i, h * LANE + j] = 1.0
    for h in range(GLA_HEADS):
        for j in range(GLA_DK):
            src = j + quarter if (j % (2 * quarter)) < quarter else j - quarter
            pe[h * GLA_DK + src, h * LANE + j] = 1.0
    return jnp.asarray(e, BF16), jnp.asarray(pe, BF16)


def _alpha_weights(w_alpha2, b_alpha):
    w = jnp.zeros((LANE, 2, GLA_HEADS, LANE), F32)
    b = jnp.zeros((2, GLA_HEADS, LANE), F32)
    for r in range(2):
        wr = w_alpha2[r].reshape(GLA_RANK, GLA_HEADS, GLA_DK)
        w = w.at[r * GLA_RANK:(r + 1) * GLA_RANK, r, :, :GLA_DK].set(wr)
        b = b.at[r, :, :GLA_DK].set(b_alpha[r].reshape(GLA_HEADS, GLA_DK))
    return w.reshape(LANE, 2 * GLA_PAD), b.reshape(1, 2 * GLA_PAD)


def kernel(x, c, ctx, c_ctx, w_mod, b_mod, norm_g, w_in, w_alpha2, b_alpha, gla_norm_g, b_gates,
           conv_w, conv_b, rpb, w_merge, b_merge, w_proj_a, w_proj_b, w_proj_c, w_out, final_g):
    bsz, seq, d = x.shape
    ctx_len = ctx.shape[1]
    depth = w_mod.shape[0]
    n_lat_rows = bsz * seq
    r_all = n_lat_rows + bsz * ctx_len
    tr = min(256, ctx_len)
    assert bsz < 8 and seq % tr == 0 and ctx_len % tr == 0 and tr % ML_CHUNK == 0
    assert seq % (GRID_W * NA_ROWS) == 0 and seq // GRID_W >= NA_KH and n_lat_rows % ctx_len == 0

    lat_tiles = seq // tr
    group_tr = lambda i: jnp.minimum(i // lat_tiles, bsz)
    n_lat_t, ctx_tiles = n_lat_rows // tr, ctx_len // tr
    pos_tile = lambda i: jnp.where(i < n_lat_t, i % lat_tiles, lat_tiles + (i - n_lat_t) % ctx_tiles)
    ti = np.arange(r_all // tr)
    in_lat = ti < n_lat_t
    first = np.where(in_lat, ti % lat_tiles == 0, (ti - n_lat_t) % ctx_tiles == 0)
    last = np.where(in_lat, ti % lat_tiles == lat_tiles - 1, (ti - n_lat_t) % ctx_tiles == ctx_tiles - 1)
    edge_flags = jnp.asarray(np.stack([first, last], axis=1).astype(np.int32))

    cc = jnp.zeros((8, d), F32).at[:bsz].set(c).at[bsz].set(c_ctx)
    mod = _modulation(cc, w_mod, b_mod)
    cos_t, sin_t = _rope_tables(seq, ctx_len)
    e_mat, pe_mat = _rope_mats()
    w_in_t = jnp.swapaxes(w_in, 1, 2)

    xall = jnp.concatenate([x.reshape(n_lat_rows, d), ctx.reshape(bsz * ctx_len, d)], axis=0)
    for l in range(depth):
        last_layer = l == depth - 1
        shift = mod[l, :, :d].reshape(8, 1, d)
        scale = mod[l, :, d:2 * d].reshape(8, 1, d)
        gate = mod[l, :, 2 * d:]
        w_small = jnp.zeros((LANE, d), F32)
        w_small = w_small.at[:2 * GLA_RANK].set(w_in_t[l, OFF_AA:OFF_B])
        w_small = w_small.at[2 * GLA_RANK:2 * GLA_RANK + 4 * ML_HEADS].set(w_in_t[l, OFF_GATES:OFF_C])
        h, small = _norm_mod(xall, norm_g[l], shift, scale, w_small.T, tr, group_tr)

        p_a = _mm_t(h, w_in_t, l, 0, OFF_AA, r_all, name="in_proj_a")
        p_b = _mm_t(h, w_in_t, l, OFF_B, 5 * B_W, r_all, name="in_proj_b")
        p_c = _mm_t(h, w_in_t, l, OFF_C, 4 * C_W, r_all, name="in_proj_c")

        w2pad, b2pad = _alpha_weights(w_alpha2[l], b_alpha[l])
        qr, kr, bcum = _gla_prep(p_a, small, w2pad, b2pad, e_mat, pe_mat, cos_t, sin_t, tr, pos_tile)
        y_f, y_b = _gla_scan(qr, kr, p_a, bcum, bsz, seq, ctx_len)

        qk_b, kt_b = _ml_prep(p_b, conv_w[l], conv_b[l], tr, edge_flags)
        g = small[:, 2 * GLA_RANK:2 * GLA_RANK + 4 * ML_HEADS].reshape(r_all, 4, ML_HEADS) + b_gates[l]
        g = jnp.transpose(g, (1, 2, 0))
        h_f, h_b = _ml_scan(qk_b, kt_b, p_b, g[0::2], g[1::2], bsz, seq, ctx_len)
        ya, yb = _post(y_f, y_b, p_a, gla_norm_g[l], h_f, h_b, p_b, tr)

        yc = _na(p_c, _na_bias_table(rpb[l]), bsz, seq, ctx_len, with_ctx=not last_layer)

        n_rows = n_lat_rows if last_layer else r_all
        gates = _mm_gate(h, w_merge, b_merge.reshape(depth, 1, -1), l, n_rows, name="merge_gates")
        u = _proj_merge(ya, yb, yc, w_proj_a, w_proj_b, w_proj_c, l, gates, n_rows)
        xall = _out_proj(u, w_out, l, xall, gate, n_rows, bsz, seq)

    out = _final_norm(xall, final_g, n_lat_rows, tr)
    return out.reshape(bsz, seq, d)
```

```python
import functools

import numpy as np
import jax
import jax.numpy as jnp
from jax import lax
from jax.experimental import pallas as pl
from jax.experimental.pallas import tpu as pltpu

F32 = jnp.float32
BF16 = jnp.bfloat16
HI = lax.Precision.HIGHEST

GRID_W = 64
EPS = 1e-6
ROPE_BASE = 10000.0
GLA_HEADS, GLA_DK, GLA_DV, GLA_RANK, GLA_TAU = 8, 64, 128, 16, 16.0
ML_HEADS, ML_DH = 8, 128
NA_HEADS, NA_DH, NA_KH, NA_KW = 16, 128, 8, 16
A_QK = GLA_HEADS * GLA_DK
A_V = GLA_HEADS * GLA_DV
B_W = ML_HEADS * ML_DH
C_W = NA_HEADS * NA_DH
LANE = 128
GLA_PAD = GLA_HEADS * LANE
GLA_CHUNK = 64
GLA_SUB = 16
ML_CHUNK = 128
NA_ROWS = 8
NEG = -1e30
OFF_AA = 2 * A_QK + 2 * A_V
OFF_B = OFF_AA + 2 * GLA_RANK
OFF_GATES = OFF_B + 5 * B_W
OFF_C = OFF_GATES + 4 * ML_HEADS
SUBLANE = 8
BF16_ROWS = 16
TILE_CANDIDATES = (512, 256, 128)
MAX_ROW_TILE = 1152
V7X_VMEM_BYTES = 64 * 1024 * 1024
VMEM_LIMIT = V7X_VMEM_BYTES * 13 // 16
MOD_ROWS = SUBLANE


def _cp(*sem):
    return pltpu.CompilerParams(dimension_semantics=sem, vmem_limit_bytes=VMEM_LIMIT)


def _nt(a, b):
    return lax.dot_general(a, b, (((1,), (1,)), ((), ())), preferred_element_type=F32)


def _tn(a, b):
    return lax.dot_general(a, b, (((0,), (0,)), ((), ())), preferred_element_type=F32)


def _dot_split(a, b):
    a_hi = a.astype(BF16)
    a_lo = (a - a_hi.astype(F32)).astype(BF16)
    b_hi = b.astype(BF16)
    b_lo = (b - b_hi.astype(F32)).astype(BF16)
    d = lambda x, y: jnp.dot(x, y, preferred_element_type=F32)
    return d(a_hi, b_hi) + (d(a_hi, b_lo) + d(a_lo, b_hi))


def _log_sigmoid(x):
    return jnp.minimum(x, 0.0) - jnp.log1p(jnp.exp(-jnp.abs(x)))


def _silu(x):
    return x * jax.nn.sigmoid(x)


def _pick(n, cands):
    for c in cands:
        if n % c == 0:
            return c
    raise ValueError(f"no tile for {n}")


def _mod_kernel(s_ref, w_ref, b_ref, o_ref, *, kc):
    s = _silu(s_ref[...]).astype(BF16)
    d = s.shape[1]
    acc = jnp.zeros(o_ref.shape[1:], F32)
    for k0 in range(0, d, kc):
        acc = acc + jnp.dot(s[:, k0:k0 + kc], w_ref[0, k0:k0 + kc, :].astype(BF16),
                            preferred_element_type=F32)
    o_ref[0] = acc + b_ref[0]


def _modulation(cc, w_mod, b_mod):
    depth, d, n = w_mod.shape
    tn = _pick(n, TILE_CANDIDATES)
    kc = _pick(d, TILE_CANDIDATES)
    return pl.pallas_call(
        functools.partial(_mod_kernel, kc=kc),
        grid=(depth, n // tn),
        in_specs=[pl.BlockSpec((MOD_ROWS, d), lambda l, j: (0, 0)),
                  pl.BlockSpec((1, d, tn), lambda l, j: (l, 0, j)),
                  pl.BlockSpec((1, 1, tn), lambda l, j: (l, 0, j))],
        out_specs=pl.BlockSpec((1, MOD_ROWS, tn), lambda l, j: (l, 0, j)),
        out_shape=jax.ShapeDtypeStruct((depth, MOD_ROWS, n), F32),
        compiler_params=_cp("arbitrary", "arbitrary"),
        name="modulation",
    )(cc, w_mod, b_mod.reshape(depth, 1, n))


def _norm_kernel(x_ref, g_ref, sh_ref, sc_ref, ws_ref, h_ref, sm_ref):
    x = x_ref[...]
    xn = x * lax.rsqrt(jnp.mean(x * x, axis=1, keepdims=True) + EPS)
    h = xn * g_ref[...] * (1.0 + sc_ref[0]) + sh_ref[0]
    h_ref[...] = h.astype(BF16)
    sm_ref[...] = _dot_split(h, ws_ref[...])


def _norm_mod(xall, g, shift, scale, w_small, tr, group_of):
    r, d = xall.shape
    return pl.pallas_call(
        _norm_kernel,
        grid=(r // tr,),
        in_specs=[pl.BlockSpec((tr, d), lambda i: (i, 0)),
                  pl.BlockSpec((1, d), lambda i: (0, 0)),
                  pl.BlockSpec((1, 1, d), lambda i: (group_of(i), 0, 0)),
                  pl.BlockSpec((1, 1, d), lambda i: (group_of(i), 0, 0)),
                  pl.BlockSpec((d, LANE), lambda i: (0, 0))],
        out_specs=[pl.BlockSpec((tr, d), lambda i: (i, 0)),
                   pl.BlockSpec((tr, LANE), lambda i: (i, 0))],
        out_shape=[jax.ShapeDtypeStruct((r, d), BF16),
                   jax.ShapeDtypeStruct((r, LANE), F32)],
        compiler_params=_cp("arbitrary"),
        name="norm_mod",
    )(xall, g.reshape(1, d), shift, scale, w_small)


def _final_norm_kernel(x_ref, g_ref, o_ref):
    x = x_ref[...]
    o_ref[...] = x * lax.rsqrt(jnp.mean(x * x, axis=1, keepdims=True) + EPS) * g_ref[...]


def _final_norm(x, g, n_rows, tr):
    d = x.shape[1]
    return pl.pallas_call(
        _final_norm_kernel,
        grid=(n_rows // tr,),
        in_specs=[pl.BlockSpec((tr, d), lambda i: (i, 0)),
                  pl.BlockSpec((1, d), lambda i: (0, 0))],
        out_specs=pl.BlockSpec((tr, d), lambda i: (i, 0)),
        out_shape=jax.ShapeDtypeStruct((n_rows, d), F32),
        compiler_params=_cp("arbitrary"),
        name="final_norm",
    )(x, g.reshape(1, d))


def _row_tile(n_rows):
    for t in range(MAX_ROW_TILE, LANE - 1, -BF16_ROWS):
        if n_rows % t == 0:
            return t
    raise ValueError(f"no row tile for {n_rows}")


def _cast_weight(w_ref, wb_ref):
    k = wb_ref.shape[0]
    kc = _pick(k, TILE_CANDIDATES)
    for k0 in range(0, k, kc):
        wb_ref[k0:k0 + kc, :] = w_ref[0, k0:k0 + kc, :].astype(BF16)


def _mm_gate_kernel(x_ref, w_ref, b_ref, o_ref, wb_ref):
    @pl.when(pl.program_id(1) == 0)
    def _():
        _cast_weight(w_ref, wb_ref)

    acc = jnp.dot(x_ref[...], wb_ref[...], preferred_element_type=F32)
    o_ref[...] = jax.nn.sigmoid(acc + b_ref[0]).astype(o_ref.dtype)


def _mm_gate(x, w, bias, layer, n_rows, name):
    k, n = w.shape[1:]
    tn = _pick(n, TILE_CANDIDATES)
    tm = _row_tile(n_rows)
    return pl.pallas_call(
        _mm_gate_kernel,
        grid=(n // tn, n_rows // tm),
        in_specs=[pl.BlockSpec((tm, k), lambda j, i: (i, 0)),
                  pl.BlockSpec((1, k, tn), lambda j, i: (layer, 0, j)),
                  pl.BlockSpec((1, 1, tn), lambda j, i: (layer, 0, j))],
        out_specs=pl.BlockSpec((tm, tn), lambda j, i: (i, j)),
        out_shape=jax.ShapeDtypeStruct((n_rows, n), BF16),
        scratch_shapes=[pltpu.VMEM((k, tn), BF16)],
        compiler_params=_cp("arbitrary", "arbitrary"),
        name=name,
    )(x, w, bias)


def _mm_t_kernel(x_ref, w_ref, o_ref, wb_ref):
    @pl.when(pl.program_id(1) == 0)
    def _():
        for r0 in range(0, wb_ref.shape[0], LANE):
            wb_ref[r0:r0 + LANE, :] = w_ref[0, r0:r0 + LANE, :].astype(BF16)

    o_ref[...] = _nt(x_ref[...], wb_ref[...]).astype(o_ref.dtype)


def _mm_t(x, w_t, layer, row0, n, n_rows, name):
    k = w_t.shape[2]
    tn = _pick(n, TILE_CANDIDATES)
    tm = _row_tile(n_rows)
    assert row0 % SUBLANE == 0
    if row0 % tn == 0:
        wspec = pl.BlockSpec((1, tn, k), lambda j, i: (layer, row0 // tn + j, 0))
    else:
        wspec = pl.BlockSpec((pl.Element(1), pl.Element(tn), pl.Element(k)),
                             lambda j, i: (layer, pl.multiple_of(row0 + j * tn, SUBLANE), 0))
    return pl.pallas_call(
        _mm_t_kernel,
        grid=(n // tn, n_rows // tm),
        in_specs=[pl.BlockSpec((tm, k), lambda j, i: (i, 0)), wspec],
        out_specs=pl.BlockSpec((tm, tn), lambda j, i: (i, j)),
        out_shape=jax.ShapeDtypeStruct((n_rows, n), BF16),
        scratch_shapes=[pltpu.VMEM((tn, k), BF16)],
        compiler_params=_cp("arbitrary", "arbitrary"),
        name=name,
    )(x, w_t)


def _proj_kernel(ya_ref, yb_ref, yc_ref, wa_ref, wb_ref, wc_ref, ga_ref, gb_ref, gc_ref,
                 o_ref, wa_s, wb_s, wc_s):
    @pl.when(pl.program_id(1) == 0)
    def _():
        _cast_weight(wa_ref, wa_s)
        _cast_weight(wb_ref, wb_s)
        _cast_weight(wc_ref, wc_s)

    u = ga_ref[...].astype(F32) * jnp.dot(ya_ref[...], wa_s[...], preferred_element_type=F32)
    u = u + gb_ref[...].astype(F32) * jnp.dot(yb_ref[...], wb_s[...], preferred_element_type=F32)
    u = u + gc_ref[...].astype(F32) * jnp.dot(yc_ref[...], wc_s[...], preferred_element_type=F32)
    o_ref[...] = u.astype(o_ref.dtype)


def _proj_merge(ya, yb, yc, wa, wb, wc, layer, gates, n_rows):
    d = wa.shape[2]
    tn = _pick(d, TILE_CANDIDATES)
    tm = _row_tile(n_rows)
    nj = d // tn
    row = lambda j, i: (i, 0)
    col = lambda j, i: (layer, 0, j)
    return pl.pallas_call(
        _proj_kernel,
        grid=(nj, n_rows // tm),
        in_specs=[pl.BlockSpec((tm, ya.shape[1]), row),
                  pl.BlockSpec((tm, yb.shape[1]), row),
                  pl.BlockSpec((tm, yc.shape[1]), row),
                  pl.BlockSpec((1, wa.shape[1], tn), col),
                  pl.BlockSpec((1, wb.shape[1], tn), col),
                  pl.BlockSpec((1, wc.shape[1], tn), col),
                  pl.BlockSpec((tm, tn), lambda j, i: (i, j)),
                  pl.BlockSpec((tm, tn), lambda j, i: (i, nj + j)),
                  pl.BlockSpec((tm, tn), lambda j, i: (i, 2 * nj + j))],
        out_specs=pl.BlockSpec((tm, tn), lambda j, i: (i, j)),
        out_shape=jax.ShapeDtypeStruct((n_rows, d), BF16),
        scratch_shapes=[pltpu.VMEM((wa.shape[1], tn), BF16),
                        pltpu.VMEM((wb.shape[1], tn), BF16),
                        pltpu.VMEM((wc.shape[1], tn), BF16)],
        compiler_params=_cp("arbitrary", "arbitrary"),
        name="proj_merge",
    )(ya, yb, yc, wa, wb, wc, gates, gates, gates)


def _out_kernel(u_ref, w_ref, x_ref, g_ref, o_ref, wb_ref, *, bsz, seq):
    @pl.when(pl.program_id(1) == 0)
    def _():
        _cast_weight(w_ref, wb_ref)

    acc = jnp.dot(u_ref[...], wb_ref[...], preferred_element_type=F32)
    tm = acc.shape[0]
    row = pl.program_id(1) * tm + lax.broadcasted_iota(jnp.int32, (tm, 1), 0)
    g = g_ref[bsz:bsz + 1, :]
    for b in reversed(range(bsz)):
        g = jnp.where(row < (b + 1) * seq, g_ref[b:b + 1, :], g)
    o_ref[...] = x_ref[...] + g * acc


def _out_proj(u, w, layer, xall, gate, n_rows, bsz, seq):
    k, n = w.shape[1:]
    tn = _pick(n, TILE_CANDIDATES)
    tm = _row_tile(n_rows)
    return pl.pallas_call(
        functools.partial(_out_kernel, bsz=bsz, seq=seq),
        grid=(n // tn, n_rows // tm),
        in_specs=[pl.BlockSpec((tm, k), lambda j, i: (i, 0)),
                  pl.BlockSpec((1, k, tn), lambda j, i: (layer, 0, j)),
                  pl.BlockSpec((tm, tn), lambda j, i: (i, j)),
                  pl.BlockSpec((MOD_ROWS, tn), lambda j, i: (0, j))],
        out_specs=pl.BlockSpec((tm, tn), lambda j, i: (i, j)),
        out_shape=jax.ShapeDtypeStruct((n_rows, n), F32),
        scratch_shapes=[pltpu.VMEM((k, tn), BF16)],
        compiler_params=_cp("arbitrary", "arbitrary"),
        name="out_proj",
    )(u, w, xall, gate)


def _gla_prep_kernel(qk_ref, sm_ref, w2_ref, b2_ref, e_ref, pe_ref, cos_ref, sin_ref, tri_ref,
                     q_out, k_out, b_out):
    z = _dot_split(sm_ref[...], w2_ref[...]) + b2_ref[...]
    la = _log_sigmoid(z) * (1.0 / GLA_TAU)
    for d in range(2):
        rest = la[:, d * GLA_PAD:(d + 1) * GLA_PAD]
        acc = jnp.zeros(rest.shape, F32)
        for _ in range(3):
            part = rest.astype(BF16)
            acc = acc + jnp.dot(tri_ref[d], part, preferred_element_type=F32)
            rest = rest - part.astype(F32)
        b_out[d] = acc
    cos = cos_ref[...]
    sin = sin_ref[...]
    e = e_ref[...]
    pe = pe_ref[...]
    q = qk_ref[:, :A_QK]
    k = qk_ref[:, A_QK:]
    rope = lambda t: (jnp.dot(t, e, preferred_element_type=F32) * cos
                      + jnp.dot(t, pe, preferred_element_type=F32) * sin)
    q_out[...] = (rope(q) * (GLA_DK ** -0.5)).astype(BF16)
    k_out[...] = rope(k).astype(BF16)


def _gla_prep(p_a, small, w2pad, b2pad, e_mat, pe_mat, cos_t, sin_t, tr, pos_tile):
    r = p_a.shape[0]
    full = lambda i: (0, 0)
    idx = np.arange(tr)
    same_chunk = (idx[None, :] // GLA_CHUNK) == (idx[:, None] // GLA_CHUNK)
    tri = np.stack([same_chunk & (idx[None, :] <= idx[:, None]),
                    same_chunk & (idx[None, :] >= idx[:, None])]).astype(np.float32)
    tri = jnp.asarray(tri, BF16)
    return pl.pallas_call(
        _gla_prep_kernel,
        grid=(r // tr,),
        in_specs=[pl.BlockSpec((tr, 2 * A_QK), lambda i: (i, 0)),
                  pl.BlockSpec((tr, LANE), lambda i: (i, 0)),
                  pl.BlockSpec(w2pad.shape, full),
                  pl.BlockSpec(b2pad.shape, full),
                  pl.BlockSpec(e_mat.shape, full),
                  pl.BlockSpec(pe_mat.shape, full),
                  pl.BlockSpec((tr, GLA_PAD), lambda i: (pos_tile(i), 0)),
                  pl.BlockSpec((tr, GLA_PAD), lambda i: (pos_tile(i), 0)),
                  pl.BlockSpec(tri.shape, lambda i: (0, 0, 0))],
        out_specs=[pl.BlockSpec((tr, GLA_PAD), lambda i: (i, 0)),
                   pl.BlockSpec((tr, GLA_PAD), lambda i: (i, 0)),
                   pl.BlockSpec((2, tr, GLA_PAD), lambda i: (0, i, 0))],
        out_shape=[jax.ShapeDtypeStruct((r, GLA_PAD), BF16),
                   jax.ShapeDtypeStruct((r, GLA_PAD), BF16),
                   jax.ShapeDtypeStruct((2, r, GLA_PAD), F32)],
        compiler_params=_cp("arbitrary"),
        name="gla_prep",
    )(p_a, small, w2pad, b2pad, e_mat, pe_mat, cos_t, sin_t, tri)


def _gla_masks(rev):
    c = GLA_CHUNK
    idx = np.arange(c)
    pos = c - 1 - idx if rev else idx
    rb = pos // GLA_SUB
    rh = pos // (2 * GLA_SUB)
    md = (pos[None, :] <= pos[:, None]) & (rb[None, :] == rb[:, None])
    m16 = (rh[None, :] == rh[:, None]) & (rb[None, :] < rb[:, None])
    return np.stack([md, m16]).astype(np.float32)


def _gla_scan_kernel(qf_ref, kf_ref, vf_ref, bf_ref, qb_ref, kb_ref, vb_ref, bb_ref, mask_ref,
                     yf_ref, yb_ref, st_ref, dec_s, att_s, yi_s, *ops_s):
    sub = GLA_SUB
    nb = GLA_CHUNK // sub
    assert nb == 4

    @pl.when(pl.program_id(1) == 0)
    def _():
        st_ref[...] = jnp.zeros(st_ref.shape, F32)

    rows = lambda kb: slice(kb * sub, (kb + 1) * sub)
    cat = lambda parts: jnp.concatenate(parts, axis=0)
    zeros = jnp.zeros((sub, LANE), BF16)
    heads = [slice(h * LANE, (h + 1) * LANE) for h in range(GLA_HEADS)]
    qd_s, kd_s, k16_s, q32_s, k32_s, qi_s, ka_s = ops_s
    streams = [(0, False, qf_ref, kf_ref, vf_ref, bf_ref, yf_ref),
               (1, True, qb_ref, kb_ref, vb_ref, bb_ref, yb_ref)]
    for d, rev, q_ref, k_ref, v_ref, b_ref, y_ref in streams:
        rank = lambda kb: nb - 1 - kb if rev else kb
        by_rank = lambda rk: nb - 1 - rk if rev else rk
        for h, sl in enumerate(heads):
            b = b_ref[0, :, sl]
            q = q_ref[:, sl]
            k = k_ref[:, sl]

            def end(rk):
                kb = by_rank(rk)
                r0 = kb * sub if rev else kb * sub + sub - 1
                return b[r0:r0 + 1]

            tot = end(nb - 1)
            dq = cat([b[rows(kb)] - end(rank(kb) - 1) if rank(kb) > 0 else b[rows(kb)]
                      for kb in range(nb)])
            qd_s[d, h] = (q * jnp.exp(dq)).astype(BF16)
            kd_s[d, h] = (k * jnp.exp(-dq)).astype(BF16)
            k16_s[d, h] = cat([(k[rows(kb)] * jnp.exp(end(rank(kb)) - b[rows(kb)])).astype(BF16)
                               if rank(kb) % 2 == 0 else zeros for kb in range(nb)])
            half_end = end(nb // 2 - 1)
            q32_s[d, h] = cat([(q[rows(kb)] * jnp.exp(b[rows(kb)] - half_end)).astype(BF16)
                               if rank(kb) >= nb // 2 else zeros for kb in range(nb)])
            k32_s[d, h] = cat([(k[rows(kb)] * jnp.exp(half_end - b[rows(kb)])).astype(BF16)
                               if rank(kb) < nb // 2 else zeros for kb in range(nb)])
            qi_s[d, h] = (q * jnp.exp(b)).astype(BF16)
            ka_s[d, h] = (k * jnp.exp(tot - b)).astype(BF16)
            dec_s[d, h] = jnp.exp(tot)
    for d, rev, q_ref, k_ref, v_ref, b_ref, y_ref in streams:
        md = mask_ref[d, 0] > 0.0
        m16 = mask_ref[d, 1]
        for h, sl in enumerate(heads):
            qd = qd_s[d, h]
            att = (jnp.where(md, _nt(qd, kd_s[d, h]), 0.0) + m16 * _nt(qd, k16_s[d, h])
                   + _nt(q32_s[d, h], k32_s[d, h]))
            att_s[d, h] = att.astype(BF16)
            yi_s[d, h] = _nt(qi_s[d, h], st_ref[d, h].astype(BF16))
    for d, rev, q_ref, k_ref, v_ref, b_ref, y_ref in streams:
        for h, sl in enumerate(heads):
            v = v_ref[:, sl]
            y_ref[:, sl] = (jnp.dot(att_s[d, h], v, preferred_element_type=F32)
                            + yi_s[d, h]).astype(y_ref.dtype)
            st_ref[d, h] = st_ref[d, h] * dec_s[d, h] + _tn(v, ka_s[d, h])


def _chunk_index(bsz, n_lat, n_ctx, lat0, ctx0):
    def f(b, d, c):
        in_ctx = c < n_ctx
        cc = jnp.where(d == 0, c, n_ctx - 1 - c)
        cl = jnp.where(d == 0, c - n_ctx, n_lat - 1 - (c - n_ctx))
        return jnp.where(in_ctx, ctx0 + b * n_ctx + cc, lat0 + b * n_lat + cl)
    return f


def _gla_scan(qr, kr, p_a, bcum, bsz, seq, ctx_len):
    r = qr.shape[0]
    c = GLA_CHUNK
    masks = jnp.asarray(np.stack([_gla_masks(False), _gla_masks(True)]))
    n_lat, n_ctx = seq // c, ctx_len // c
    blk = _chunk_index(bsz, n_lat, n_ctx, 0, bsz * n_lat)
    v_col = (2 * A_QK) // A_V
    nh = GLA_HEADS

    def stream(d):
        return [pl.BlockSpec((c, GLA_PAD), lambda b, i: (blk(b, d, i), 0)),
                pl.BlockSpec((c, GLA_PAD), lambda b, i: (blk(b, d, i), 0)),
                pl.BlockSpec((c, A_V), lambda b, i: (blk(b, d, i), v_col)),
                pl.BlockSpec((1, c, GLA_PAD), lambda b, i: (d, blk(b, d, i), 0))]

    out = lambda d: pl.BlockSpec((c, A_V), lambda b, i: (blk(b, d, i), 0))
    return pl.pallas_call(
        _gla_scan_kernel,
        grid=(bsz, n_lat + n_ctx),
        in_specs=stream(0) + stream(1) + [pl.BlockSpec(masks.shape, lambda b, i: (0, 0, 0, 0))],
        out_specs=[out(0), out(1)],
        out_shape=[jax.ShapeDtypeStruct((r, A_V), BF16)] * 2,
        scratch_shapes=[pltpu.VMEM((2, nh, GLA_DV, LANE), F32),
                        pltpu.VMEM((2, nh, 1, LANE), F32),
                        pltpu.VMEM((2, nh, c, c), BF16),
                        pltpu.VMEM((2, nh, c, GLA_DV), F32)]
        + [pltpu.VMEM((2, nh, c, LANE), BF16)] * 7,
        compiler_params=_cp("arbitrary", "arbitrary"),
        name="gla_scan",
    )(qr, kr, p_a, bcum, qr, kr, p_a, bcum, masks)


def _ml_prep_kernel(x_ref, xp_ref, xn_ref, w_ref, b_ref, first_ref, o_ref, kt_ref):
    i = pl.program_id(0)
    x = x_ref[...].astype(F32)
    tr = x.shape[0]
    is_first = first_ref[i, 0] > 0
    is_last = first_ref[i, 1] > 0
    prev = jnp.where(is_first, 0.0, xp_ref[SUBLANE - 1:SUBLANE, :].astype(F32))
    nxt = jnp.where(is_last, 0.0, xn_ref[0:1, :].astype(F32))
    row = lax.broadcasted_iota(jnp.int32, x.shape, 0)
    xm1 = jnp.where(row == 0, prev, pltpu.roll(x, 1, axis=0))
    xp1 = jnp.where(row == tr - 1, nxt, pltpu.roll(x, tr - 1, axis=0))
    y = xm1 * w_ref[0:1, :] + x * w_ref[1:2, :] + xp1 * w_ref[2:3, :] + b_ref[...]
    y = _silu(y)
    k = y[:, B_W:] * (ML_DH ** -0.5)
    o_ref[:, :B_W] = y[:, :B_W].astype(BF16)
    o_ref[:, B_W:] = k.astype(BF16)
    kt_ref[...] = k.T.astype(BF16)


def _ml_prep(p_b, conv_w, conv_b, tr, edge_flags):
    r = p_b.shape[0]
    w = 2 * B_W
    nb = r // SUBLANE
    per_tile = tr // SUBLANE
    return pl.pallas_call(
        _ml_prep_kernel,
        grid=(r // tr,),
        in_specs=[pl.BlockSpec((tr, w), lambda i: (i, 0)),
                  pl.BlockSpec((SUBLANE, w), lambda i: (jnp.maximum(i * per_tile - 1, 0), 0)),
                  pl.BlockSpec((SUBLANE, w), lambda i: (jnp.minimum((i + 1) * per_tile, nb - 1), 0)),
                  pl.BlockSpec((SUBLANE, w), lambda i: (0, 0)),
                  pl.BlockSpec((1, w), lambda i: (0, 0)),
                  pl.BlockSpec(memory_space=pltpu.SMEM)],
        out_specs=[pl.BlockSpec((tr, w), lambda i: (i, 0)),
                   pl.BlockSpec((B_W, tr), lambda i: (0, i))],
        out_shape=[jax.ShapeDtypeStruct((r, w), BF16),
                   jax.ShapeDtypeStruct((B_W, r), BF16)],
        compiler_params=_cp("arbitrary"),
        name="mlstm_prep",
    )(p_b, p_b, p_b, jnp.pad(conv_w, ((0, SUBLANE - conv_w.shape[0]), (0, 0))), conv_b.reshape(1, w),
      edge_flags)


def _ml_scan_kernel(qkf_ref, ktf_ref, vf_ref, gif_ref, gff_ref, qkb_ref, ktb_ref, vb_ref, gib_ref,
                    gfb_ref, tri_ref, yf_ref, yb_ref, s_ref, m_ref, qk_s, qs_s, pk_s, wi_s, kw_s):
    c = ML_CHUNK
    nh = ML_HEADS
    assert c == ML_DH

    @pl.when(pl.program_id(1) == 0)
    def _():
        s_ref[...] = jnp.zeros(s_ref.shape, F32)
        m_ref[...] = jnp.zeros(m_ref.shape, F32)

    lane = lax.broadcasted_iota(jnp.int32, (nh, c), 1)
    ones = jnp.ones((c, ML_DH), BF16)
    heads = [slice(h * ML_DH, (h + 1) * ML_DH) for h in range(nh)]
    streams = [(0, qkf_ref, ktf_ref, vf_ref, gif_ref, gff_ref, yf_ref),
               (1, qkb_ref, ktb_ref, vb_ref, gib_ref, gfb_ref, yb_ref)]
    gates = []
    for d, qk_ref, kt_ref, v_ref, git_ref, gft_ref, y_ref in streams:
        tri = tri_ref[d]
        lft = _log_sigmoid(gft_ref[0])
        brow = lax.dot_general(lft, tri, (((1,), (1,)), ((), ())), precision=HI,
                               preferred_element_type=F32)
        tot = jnp.sum(lft, axis=1, keepdims=True)
        g = git_ref[0] - brow
        run = g
        sh = 1
        while sh < c:
            if d == 0:
                run = jnp.maximum(run, jnp.where(lane >= sh, pltpu.roll(run, sh, axis=1), NEG))
            else:
                run = jnp.maximum(run, jnp.where(lane < c - sh, pltpu.roll(run, c - sh, axis=1), NEG))
            sh *= 2
        m_prev = m_ref[d]
        big_m = jnp.maximum(m_prev, run)
        m_next = jnp.maximum(m_prev, jnp.max(g, axis=1, keepdims=True))
        decay = jnp.exp(m_prev - m_next)
        w_rows = jnp.exp(g - m_next)
        cols = jnp.concatenate([big_m, brow + big_m, jnp.zeros((c - 2 * nh, c), F32)], axis=0).T
        gates.append((g, m_prev, decay, w_rows, cols))
        m_ref[d] = tot + m_next
    for d, qk_ref, kt_ref, v_ref, git_ref, gft_ref, y_ref in streams:
        for h, sl in enumerate(heads):
            q = qk_ref[:, sl]
            qk_s[d, h] = jnp.dot(q, kt_ref[sl, :], preferred_element_type=F32)
            qs_s[d, h] = jnp.dot(q, s_ref[d, h].astype(BF16), preferred_element_type=F32)
    for d, qk_ref, kt_ref, v_ref, git_ref, gft_ref, y_ref in streams:
        g, m_prev, decay, w_rows, cols = gates[d]
        causal = tri_ref[d] > 0.0
        for h, sl in enumerate(heads):
            m_mat = jnp.broadcast_to(cols[:, h:h + 1], (c, c))
            p = jnp.where(causal, jnp.exp(g[h:h + 1, :] - m_mat), 0.0)
            pk_s[d, h] = (qk_s[d, h] * p).astype(BF16)
            wi_s[d, h] = jnp.exp(m_prev[h:h + 1, :] - m_mat)
            kw_s[d, h] = (kt_ref[sl, :].astype(F32) * w_rows[h:h + 1, :]).astype(BF16)
    for d, qk_ref, kt_ref, v_ref, git_ref, gft_ref, y_ref in streams:
        g, m_prev, decay, w_rows, cols = gates[d]
        for h, sl in enumerate(heads):
            vo = jnp.concatenate([v_ref[:, sl], ones], axis=1)
            w_inter = wi_s[d, h]
            nd = (jnp.dot(pk_s[d, h], vo, preferred_element_type=F32)
                  + jnp.concatenate([w_inter, w_inter], axis=1) * qs_s[d, h])
            neg_m = jnp.exp(-jnp.broadcast_to(cols[:, nh + h:nh + h + 1], (c, ML_DH)))
            y_ref[:, sl] = (nd[:, :ML_DH] / jnp.maximum(jnp.abs(nd[:, ML_DH:]), neg_m)).astype(y_ref.dtype)
            s_ref[d, h] = (decay[h:h + 1, :] * s_ref[d, h]
                           + jnp.dot(kw_s[d, h], vo, preferred_element_type=F32))


def _ml_scan(qk, kt, p_b, git, gft, bsz, seq, ctx_len):
    r = qk.shape[0]
    c = ML_CHUNK
    idx = np.arange(c)
    tri = np.stack([(idx[None, :] <= idx[:, None]), (idx[None, :] >= idx[:, None])]).astype(np.float32)
    n_lat, n_ctx = seq // c, ctx_len // c
    blk = _chunk_index(bsz, n_lat, n_ctx, 0, bsz * n_lat)
    h = ML_HEADS

    def stream(d):
        return [pl.BlockSpec((c, 2 * B_W), lambda b, i: (blk(b, d, i), 0)),
                pl.BlockSpec((B_W, c), lambda b, i: (0, blk(b, d, i))),
                pl.BlockSpec((c, B_W), lambda b, i: (blk(b, d, i), 2)),
                pl.BlockSpec((1, h, c), lambda b, i: (d, 0, blk(b, d, i))),
                pl.BlockSpec((1, h, c), lambda b, i: (d, 0, blk(b, d, i)))]

    out = lambda d: pl.BlockSpec((c, B_W), lambda b, i: (blk(b, d, i), 0))
    return pl.pallas_call(
        _ml_scan_kernel,
        grid=(bsz, n_lat + n_ctx),
        in_specs=stream(0) + stream(1) + [pl.BlockSpec((2, c, c), lambda b, i: (0, 0, 0))],
        out_specs=[out(0), out(1)],
        out_shape=[jax.ShapeDtypeStruct((r, B_W), BF16)] * 2,
        scratch_shapes=[pltpu.VMEM((2, h, ML_DH, 2 * ML_DH), F32),
                        pltpu.VMEM((2, h, 1), F32),
                        pltpu.VMEM((2, h, c, c), F32),
                        pltpu.VMEM((2, h, c, 2 * ML_DH), F32),
                        pltpu.VMEM((2, h, c, c), BF16),
                        pltpu.VMEM((2, h, c, c), F32),
                        pltpu.VMEM((2, h, ML_DH, c), BF16)],
        compiler_params=_cp("arbitrary", "arbitrary"),
        name="mlstm_scan",
    )(qk, kt, p_b, git, gft, qk, kt, p_b, git, gft, jnp.asarray(tri))


def _post_kernel(yf_ref, yb_ref, za_ref, g_ref, hf_ref, hb_ref, zb_ref, ob_ref, oa_ref, obb_ref):
    for h in range(GLA_HEADS):
        sl = slice(h * GLA_DV, (h + 1) * GLA_DV)
        o = yf_ref[:, sl].astype(F32) + yb_ref[:, sl].astype(F32)
        o = o * lax.rsqrt(jnp.mean(o * o, axis=1, keepdims=True) + EPS) * g_ref[:, sl]
        oa_ref[:, sl] = (o * _silu(za_ref[:, sl].astype(F32))).astype(BF16)
    hb = hf_ref[...].astype(F32) + hb_ref[...].astype(F32)
    obb_ref[...] = (jax.nn.sigmoid(ob_ref[...].astype(F32)) * hb
                    * _silu(zb_ref[...].astype(F32))).astype(BF16)


def _post(y_f, y_b, p_a, gnorm, h_f, h_b, p_b, tr):
    r = p_a.shape[0]
    za_col = (2 * A_QK + A_V) // A_V
    return pl.pallas_call(
        _post_kernel,
        grid=(r // tr,),
        in_specs=[pl.BlockSpec((tr, A_V), lambda i: (i, 0)),
                  pl.BlockSpec((tr, A_V), lambda i: (i, 0)),
                  pl.BlockSpec((tr, A_V), lambda i: (i, za_col)),
                  pl.BlockSpec((1, A_V), lambda i: (0, 0)),
                  pl.BlockSpec((tr, B_W), lambda i: (i, 0)),
                  pl.BlockSpec((tr, B_W), lambda i: (i, 0)),
                  pl.BlockSpec((tr, B_W), lambda i: (i, 3)),
                  pl.BlockSpec((tr, B_W), lambda i: (i, 4))],
        out_specs=[pl.BlockSpec((tr, A_V), lambda i: (i, 0)),
                   pl.BlockSpec((tr, B_W), lambda i: (i, 0))],
        out_shape=[jax.ShapeDtypeStruct((r, A_V), BF16),
                   jax.ShapeDtypeStruct((r, B_W), BF16)],
        compiler_params=_cp("arbitrary"),
        name="post_ab",
    )(y_f, y_b, p_a, gnorm.reshape(1, A_V), h_f, h_b, p_b, p_b)


def _na_kernel(q_ref, k_ref, v_ref, z_ref, qc_ref, kc_ref, vc_ref, zc_ref, t_ref, o_ref,
               s_ref, p_ref, *, rows, bsz, ctx_len, with_ctx):
    b = pl.program_id(0)
    scale = NA_DH ** -0.5
    if with_ctx:
        @pl.when(b == bsz)
        def _():
            for bb in range(bsz):
                rs_ = slice(bb * ctx_len, (bb + 1) * ctx_len)
                q = (qc_ref[rs_, :].astype(F32) * scale).astype(BF16)
                s = _nt(q, kc_ref[rs_, :])
                p = jnp.exp(s - jnp.max(s, axis=1, keepdims=True))
                o = (jnp.dot(p.astype(BF16), vc_ref[rs_, :], preferred_element_type=F32)
                     / jnp.sum(p, axis=1, keepdims=True))
                o_ref[rs_, :] = (o * _silu(zc_ref[rs_, :].astype(F32))).astype(BF16)
            o_ref[bsz * ctx_len:, :] = jnp.zeros((o_ref.shape[0] - bsz * ctx_len, NA_DH), BF16)

    @pl.when(b < bsz)
    def _():
        _na_latent(q_ref, k_ref, v_ref, z_ref, kc_ref, vc_ref, t_ref, o_ref, s_ref, p_ref,
                   rows=rows, c0=pl.multiple_of(b * ctx_len, ctx_len), ctx_len=ctx_len)


def _na_latent(q_ref, k_ref, v_ref, z_ref, kc_ref, vc_ref, t_ref, o_ref, s_ref, p_ref, *,
               rows, c0, ctx_len):
    kc = kc_ref[pl.ds(c0, ctx_len), :]
    vc = vc_ref[pl.ds(c0, ctx_len), :]
    win = NA_KH * GRID_W
    nq = NA_ROWS * GRID_W

    def body(t, carry):
        q0 = pl.multiple_of(t * nq, nq)
        q = (q_ref[pl.ds(q0, nq), :].astype(F32) * (NA_DH ** -0.5)).astype(BF16)
        starts = []
        for j in range(NA_ROWS):
            r = t * NA_ROWS + j
            rs = jnp.clip(r - NA_KH // 2, 0, rows - NA_KH)
            k0 = pl.multiple_of(rs * GRID_W, GRID_W)
            starts.append(k0)
            qs = slice(j * GRID_W, (j + 1) * GRID_W)
            s_ref[qs, :] = _nt(q[qs], k_ref[pl.ds(k0, win), :]) + t_ref[0, r - rs]
        s_c = _nt(q, kc)
        s_w = s_ref[...]
        m = jnp.maximum(jnp.max(s_w, axis=1, keepdims=True), jnp.max(s_c, axis=1, keepdims=True))
        p_w = jnp.exp(s_w - m)
        p_c = jnp.exp(s_c - m)
        l = jnp.sum(p_w, axis=1, keepdims=True) + jnp.sum(p_c, axis=1, keepdims=True)
        p_ref[...] = p_w.astype(BF16)
        o_c = jnp.dot(p_c.astype(BF16), vc, preferred_element_type=F32)
        o_w = jnp.concatenate(
            [jnp.dot(p_ref[j * GRID_W:(j + 1) * GRID_W, :], v_ref[pl.ds(starts[j], win), :],
                     preferred_element_type=F32) for j in range(NA_ROWS)], axis=0)
        z = z_ref[pl.ds(q0, nq), :].astype(F32)
        o_ref[pl.ds(q0, nq), :] = ((o_w + o_c) / l * _silu(z)).astype(BF16)
        return carry

    lax.fori_loop(0, rows // NA_ROWS, body, 0)


def _na_bias_table(rpb):
    c = np.arange(GRID_W)
    cs = np.clip(c - NA_KW // 2, 0, GRID_W - NA_KW)
    valid = (c[None, :] >= cs[:, None]) & (c[None, :] < cs[:, None] + NA_KW)
    dc = c[None, :] - c[:, None] + NA_KW - 1
    ndc = 2 * NA_KW - 1
    onehot = ((dc[None] == np.arange(ndc)[:, None, None]) & valid[None]).astype(np.float32)
    full = jnp.einsum("hdj,jck->hdck", rpb, jnp.asarray(onehot), precision=HI)
    full = full + jnp.asarray(np.where(valid, 0.0, NEG).astype(np.float32))
    t = jnp.stack([full[:, NA_KH - 1 - oi:2 * NA_KH - 1 - oi] for oi in range(NA_KH)], axis=1)
    t = jnp.transpose(t, (0, 1, 3, 2, 4))
    return t.reshape(NA_HEADS, NA_KH, GRID_W, NA_KH * GRID_W)


def _na(p_c, table, bsz, seq, ctx_len, with_ctx):
    rows = seq // GRID_W
    n_ctx = bsz * ctx_len
    assert (bsz * seq) % n_ctx == 0 and n_ctx <= seq
    ctx_blk = (bsz * seq) // n_ctx
    hq, hk, hv, hz = 0, NA_HEADS, 2 * NA_HEADS, 3 * NA_HEADS
    lat = lambda off: pl.BlockSpec((seq, NA_DH), lambda b, h: (jnp.minimum(b, bsz - 1), off + h))
    ctx = lambda off: pl.BlockSpec((n_ctx, NA_DH), lambda b, h: (ctx_blk, off + h))
    nb = bsz + int(with_ctx)
    return pl.pallas_call(
        functools.partial(_na_kernel, rows=rows, bsz=bsz, ctx_len=ctx_len, with_ctx=with_ctx),
        grid=(nb, NA_HEADS),
        in_specs=[lat(hq), lat(hk), lat(hv), lat(hz), ctx(hq), ctx(hk), ctx(hv), ctx(hz),
                  pl.BlockSpec((1,) + table.shape[1:], lambda b, h: (h, 0, 0, 0))],
        out_specs=pl.BlockSpec((seq, NA_DH), lambda b, h: (b, h)),
        out_shape=jax.ShapeDtypeStruct((nb * seq, C_W), BF16),
        scratch_shapes=[pltpu.VMEM((NA_ROWS * GRID_W, NA_KH * GRID_W), F32),
                        pltpu.VMEM((NA_ROWS * GRID_W, NA_KH * GRID_W), BF16)],
        compiler_params=_cp("arbitrary", "arbitrary"),
        name="nbr_attn",
    )(p_c, p_c, p_c, p_c, p_c, p_c, p_c, p_c, table)


def _rope_tables(seq, ctx_len):
    nf = GLA_DK // 4
    t = np.arange(seq)
    inv = ROPE_BASE ** (-np.arange(nf, dtype=np.float32) / nf)
    ar = (t // GRID_W).astype(np.float32)[:, None] * inv[None, :]
    ac = (t % GRID_W).astype(np.float32)[:, None] * inv[None, :]
    ar, ac = jnp.asarray(ar), jnp.asarray(ac)
    cos = jnp.concatenate([jnp.cos(ar), jnp.cos(ar), jnp.cos(ac), jnp.cos(ac)], axis=1)
    sin = jnp.concatenate([-jnp.sin(ar), jnp.sin(ar), -jnp.sin(ac), jnp.sin(ac)], axis=1)
    cos = jnp.concatenate([cos, jnp.ones((ctx_len, GLA_DK), F32)], axis=0)
    sin = jnp.concatenate([sin, jnp.zeros((ctx_len, GLA_DK), F32)], axis=0)
    pad = jnp.zeros((seq + ctx_len, LANE - GLA_DK), F32)
    cos = jnp.tile(jnp.concatenate([cos, pad], axis=1), (1, GLA_HEADS))
    sin = jnp.tile(jnp.concatenate([sin, pad], axis=1), (1, GLA_HEADS))
    return cos, sin


def _rope_mats():
    e = np.zeros((A_QK, GLA_PAD), np.float32)
    pe = np.zeros((A_QK, GLA_PAD), np.float32)
    quarter = GLA_DK // 4
    for i in range(A_QK):
        h, j = divmod(i, GLA_DK)
        e[i, h * LANE + j] = 1.0
    for h in range(GLA_HEADS):
        for j in range(GLA_DK):
            src = j + quarter if (j % (2 * quarter)) < quarter else j - quarter
            pe[h * GLA_DK + src, h * LANE + j] = 1.0
    return jnp.asarray(e, BF16), jnp.asarray(pe, BF16)


def _alpha_weights(w_alpha2, b_alpha):
    w = jnp.zeros((LANE, 2, GLA_HEADS, LANE), F32)
    b = jnp.zeros((2, GLA_HEADS, LANE), F32)
    for r in range(2):
        wr = w_alpha2[r].reshape(GLA_RANK, GLA_HEADS, GLA_DK)
        w = w.at[r * GLA_RANK:(r + 1) * GLA_RANK, r, :, :GLA_DK].set(wr)
        b = b.at[r, :, :GLA_DK].set(b_alpha[r].reshape(GLA_HEADS, GLA_DK))
    return w.reshape(LANE, 2 * GLA_PAD), b.reshape(1, 2 * GLA_PAD)


def kernel(x, c, ctx, c_ctx, w_mod, b_mod, norm_g, w_in, w_alpha2, b_alpha, gla_norm_g, b_gates,
           conv_w, conv_b, rpb, w_merge, b_merge, w_proj_a, w_proj_b, w_proj_c, w_out, final_g):
    bsz, seq, d = x.shape
    ctx_len = ctx.shape[1]
    depth = w_mod.shape[0]
    n_lat_rows = bsz * seq
    r_all = n_lat_rows + bsz * ctx_len
    tr = min(256, ctx_len)
    assert bsz < MOD_ROWS and seq % tr == 0 and ctx_len % tr == 0 and tr % ML_CHUNK == 0
    assert seq % (GRID_W * NA_ROWS) == 0 and seq // GRID_W >= NA_KH and n_lat_rows % ctx_len == 0

    lat_tiles = seq // tr
    group_tr = lambda i: jnp.minimum(i // lat_tiles, bsz)
    n_lat_t, ctx_tiles = n_lat_rows // tr, ctx_len // tr
    pos_tile = lambda i: jnp.where(i < n_lat_t, i % lat_tiles, lat_tiles + (i - n_lat_t) % ctx_tiles)
    ti = np.arange(r_all // tr)
    in_lat = ti < n_lat_t
    first = np.where(in_lat, ti % lat_tiles == 0, (ti - n_lat_t) % ctx_tiles == 0)
    last = np.where(in_lat, ti % lat_tiles == lat_tiles - 1, (ti - n_lat_t) % ctx_tiles == ctx_tiles - 1)
    edge_flags = jnp.asarray(np.stack([first, last], axis=1).astype(np.int32))

    cc = jnp.zeros((MOD_ROWS, d), F32).at[:bsz].set(c).at[bsz].set(c_ctx)
    mod = _modulation(cc, w_mod, b_mod)
    cos_t, sin_t = _rope_tables(seq, ctx_len)
    e_mat, pe_mat = _rope_mats()
    w_in_t = jnp.swapaxes(w_in, 1, 2)

    xall = jnp.concatenate([x.reshape(n_lat_rows, d), ctx.reshape(bsz * ctx_len, d)], axis=0)
    for l in range(depth):
        last_layer = l == depth - 1
        shift = mod[l, :, :d].reshape(MOD_ROWS, 1, d)
        scale = mod[l, :, d:2 * d].reshape(MOD_ROWS, 1, d)
        gate = mod[l, :, 2 * d:]
        w_small = jnp.zeros((LANE, d), F32)
        w_small = w_small.at[:2 * GLA_RANK].set(w_in_t[l, OFF_AA:OFF_B])
        w_small = w_small.at[2 * GLA_RANK:2 * GLA_RANK + 4 * ML_HEADS].set(w_in_t[l, OFF_GATES:OFF_C])
        h, small = _norm_mod(xall, norm_g[l], shift, scale, w_small.T, tr, group_tr)

        p_a = _mm_t(h, w_in_t, l, 0, OFF_AA, r_all, name="in_proj_a")
        p_b = _mm_t(h, w_in_t, l, OFF_B, 5 * B_W, r_all, name="in_proj_b")
        p_c = _mm_t(h, w_in_t, l, OFF_C, 4 * C_W, r_all, name="in_proj_c")

        w2pad, b2pad = _alpha_weights(w_alpha2[l], b_alpha[l])
        qr, kr, bcum = _gla_prep(p_a, small, w2pad, b2pad, e_mat, pe_mat, cos_t, sin_t, tr, pos_tile)
        y_f, y_b = _gla_scan(qr, kr, p_a, bcum, bsz, seq, ctx_len)

        qk_b, kt_b = _ml_prep(p_b, conv_w[l], conv_b[l], tr, edge_flags)
        g = small[:, 2 * GLA_RANK:2 * GLA_RANK + 4 * ML_HEADS].reshape(r_all, 4, ML_HEADS) + b_gates[l]
        g = jnp.transpose(g, (1, 2, 0))
        h_f, h_b = _ml_scan(qk_b, kt_b, p_b, g[0::2], g[1::2], bsz, seq, ctx_len)
        ya, yb = _post(y_f, y_b, p_a, gla_norm_g[l], h_f, h_b, p_b, tr)

        yc = _na(p_c, _na_bias_table(rpb[l]), bsz, seq, ctx_len, with_ctx=not last_layer)

        n_rows = n_lat_rows if last_layer else r_all
        gates = _mm_gate(h, w_merge, b_merge.reshape(depth, 1, -1), l, n_rows, name="merge_gates")
        u = _proj_merge(ya, yb, yc, w_proj_a, w_proj_b, w_proj_c, l, gates, n_rows)
        xall = _out_proj(u, w_out, l, xall, gate, n_rows, bsz, seq)

    out = _final_norm(xall, final_g, n_lat_rows, tr)
    return out.reshape(bsz, seq, d)
```

```python
import functools

import numpy as np
import jax
import jax.numpy as jnp
from jax import lax
from jax.experimental import pallas as pl
from jax.experimental.pallas import tpu as pltpu

F32 = jnp.float32
BF16 = jnp.bfloat16
HI = lax.Precision.HIGHEST

GRID_W = 64
EPS = 1e-6
ROPE_BASE = 10000.0
GLA_HEADS, GLA_DK, GLA_DV, GLA_RANK, GLA_TAU = 8, 64, 128, 16, 16.0
ML_HEADS, ML_DH = 8, 128
NA_HEADS, NA_DH, NA_KH, NA_KW = 16, 128, 8, 16
A_QK = GLA_HEADS * GLA_DK
A_V = GLA_HEADS * GLA_DV
B_W = ML_HEADS * ML_DH
C_W = NA_HEADS * NA_DH
LANE = 128
GLA_PAD = GLA_HEADS * LANE
GLA_CHUNK = 64
GLA_SUB = 16
ML_CHUNK = 128
NA_ROWS = 16
NEG = -1e30
OFF_AA = 2 * A_QK + 2 * A_V
OFF_B = OFF_AA + 2 * GLA_RANK
OFF_GATES = OFF_B + 5 * B_W
OFF_C = OFF_GATES + 4 * ML_HEADS
SUBLANE = 8
BF16_ROWS = 16
TILE_CANDIDATES = (512, 256, 128)
MAX_ROW_TILE = 1152
V7X_VMEM_BYTES = 64 * 1024 * 1024
VMEM_LIMIT = V7X_VMEM_BYTES * 13 // 16
MOD_ROWS = SUBLANE


def _cp(*sem):
    return pltpu.CompilerParams(dimension_semantics=sem, vmem_limit_bytes=VMEM_LIMIT)


def _nt(a, b):
    return lax.dot_general(a, b, (((1,), (1,)), ((), ())), preferred_element_type=F32)


def _tn(a, b):
    return lax.dot_general(a, b, (((0,), (0,)), ((), ())), preferred_element_type=F32)


def _dot_split(a, b):
    a_hi = a.astype(BF16)
    a_lo = (a - a_hi.astype(F32)).astype(BF16)
    b_hi = b.astype(BF16)
    b_lo = (b - b_hi.astype(F32)).astype(BF16)
    d = lambda x, y: jnp.dot(x, y, preferred_element_type=F32)
    return d(a_hi, b_hi) + (d(a_hi, b_lo) + d(a_lo, b_hi))


def _log_sigmoid(x):
    return jnp.minimum(x, 0.0) - jnp.log1p(jnp.exp(-jnp.abs(x)))


def _silu(x):
    return x * jax.nn.sigmoid(x)


def _pick(n, cands):
    for c in cands:
        if n % c == 0:
            return c
    raise ValueError(f"no tile for {n}")


def _mod_kernel(s_ref, w_ref, b_ref, o_ref, *, kc):
    s = _silu(s_ref[...]).astype(BF16)
    d = s.shape[1]
    acc = jnp.zeros(o_ref.shape[1:], F32)
    for k0 in range(0, d, kc):
        acc = acc + jnp.dot(s[:, k0:k0 + kc], w_ref[0, k0:k0 + kc, :].astype(BF16),
                            preferred_element_type=F32)
    o_ref[0] = acc + b_ref[0]


def _modulation(cc, w_mod, b_mod):
    depth, d, n = w_mod.shape
    tn = _pick(n, TILE_CANDIDATES)
    kc = _pick(d, TILE_CANDIDATES)
    return pl.pallas_call(
        functools.partial(_mod_kernel, kc=kc),
        grid=(depth, n // tn),
        in_specs=[pl.BlockSpec((MOD_ROWS, d), lambda l, j: (0, 0)),
                  pl.BlockSpec((1, d, tn), lambda l, j: (l, 0, j)),
                  pl.BlockSpec((1, 1, tn), lambda l, j: (l, 0, j))],
        out_specs=pl.BlockSpec((1, MOD_ROWS, tn), lambda l, j: (l, 0, j)),
        out_shape=jax.ShapeDtypeStruct((depth, MOD_ROWS, n), F32),
        compiler_params=_cp("arbitrary", "arbitrary"),
        name="modulation",
    )(cc, w_mod, b_mod.reshape(depth, 1, n))


def _norm_kernel(x_ref, g_ref, sh_ref, sc_ref, ws_ref, h_ref, sm_ref):
    x = x_ref[...]
    xn = x * lax.rsqrt(jnp.mean(x * x, axis=1, keepdims=True) + EPS)
    h = xn * g_ref[...] * (1.0 + sc_ref[0]) + sh_ref[0]
    h_ref[...] = h.astype(BF16)
    sm_ref[...] = _dot_split(h, ws_ref[...])


def _norm_mod(xall, g, shift, scale, w_small, tr, group_of):
    r, d = xall.shape
    return pl.pallas_call(
        _norm_kernel,
        grid=(r // tr,),
        in_specs=[pl.BlockSpec((tr, d), lambda i: (i, 0)),
                  pl.BlockSpec((1, d), lambda i: (0, 0)),
                  pl.BlockSpec((1, 1, d), lambda i: (group_of(i), 0, 0)),
                  pl.BlockSpec((1, 1, d), lambda i: (group_of(i), 0, 0)),
                  pl.BlockSpec((d, LANE), lambda i: (0, 0))],
        out_specs=[pl.BlockSpec((tr, d), lambda i: (i, 0)),
                   pl.BlockSpec((tr, LANE), lambda i: (i, 0))],
        out_shape=[jax.ShapeDtypeStruct((r, d), BF16),
                   jax.ShapeDtypeStruct((r, LANE), F32)],
        compiler_params=_cp("arbitrary"),
        name="norm_mod",
    )(xall, g.reshape(1, d), shift, scale, w_small)


def _final_norm_kernel(x_ref, g_ref, o_ref):
    x = x_ref[...]
    o_ref[...] = x * lax.rsqrt(jnp.mean(x * x, axis=1, keepdims=True) + EPS) * g_ref[...]


def _final_norm(x, g, n_rows, tr):
    d = x.shape[1]
    return pl.pallas_call(
        _final_norm_kernel,
        grid=(n_rows // tr,),
        in_specs=[pl.BlockSpec((tr, d), lambda i: (i, 0)),
                  pl.BlockSpec((1, d), lambda i: (0, 0))],
        out_specs=pl.BlockSpec((tr, d), lambda i: (i, 0)),
        out_shape=jax.ShapeDtypeStruct((n_rows, d), F32),
        compiler_params=_cp("arbitrary"),
        name="final_norm",
    )(x, g.reshape(1, d))


def _row_tile(n_rows):
    for t in range(MAX_ROW_TILE, LANE - 1, -BF16_ROWS):
        if n_rows % t == 0:
            return t
    raise ValueError(f"no row tile for {n_rows}")


def _cast_weight(w_ref, wb_ref):
    k = wb_ref.shape[0]
    kc = _pick(k, TILE_CANDIDATES)
    for k0 in range(0, k, kc):
        wb_ref[k0:k0 + kc, :] = w_ref[0, k0:k0 + kc, :].astype(BF16)


def _mm_gate_kernel(x_ref, w_ref, b_ref, o_ref, wb_ref):
    @pl.when(pl.program_id(1) == 0)
    def _():
        _cast_weight(w_ref, wb_ref)

    acc = jnp.dot(x_ref[...], wb_ref[...], preferred_element_type=F32)
    o_ref[...] = jax.nn.sigmoid(acc + b_ref[0]).astype(o_ref.dtype)


def _mm_gate(x, w, bias, layer, n_rows, name):
    k, n = w.shape[1:]
    tn = _pick(n, TILE_CANDIDATES)
    tm = _row_tile(n_rows)
    return pl.pallas_call(
        _mm_gate_kernel,
        grid=(n // tn, n_rows // tm),
        in_specs=[pl.BlockSpec((tm, k), lambda j, i: (i, 0)),
                  pl.BlockSpec((1, k, tn), lambda j, i: (layer, 0, j)),
                  pl.BlockSpec((1, 1, tn), lambda j, i: (layer, 0, j))],
        out_specs=pl.BlockSpec((tm, tn), lambda j, i: (i, j)),
        out_shape=jax.ShapeDtypeStruct((n_rows, n), BF16),
        scratch_shapes=[pltpu.VMEM((k, tn), BF16)],
        compiler_params=_cp("arbitrary", "arbitrary"),
        name=name,
    )(x, w, bias)


def _mm_t_kernel(x_ref, w_ref, o_ref, wb_ref):
    @pl.when(pl.program_id(1) == 0)
    def _():
        for r0 in range(0, wb_ref.shape[0], LANE):
            wb_ref[r0:r0 + LANE, :] = w_ref[0, r0:r0 + LANE, :].astype(BF16)

    o_ref[...] = _nt(x_ref[...], wb_ref[...]).astype(o_ref.dtype)


def _mm_t(x, w_t, layer, row0, n, n_rows, name):
    k = w_t.shape[2]
    tn = _pick(n, TILE_CANDIDATES)
    tm = _row_tile(n_rows)
    assert row0 % SUBLANE == 0
    if row0 % tn == 0:
        wspec = pl.BlockSpec((1, tn, k), lambda j, i: (layer, row0 // tn + j, 0))
    else:
        wspec = pl.BlockSpec((pl.Element(1), pl.Element(tn), pl.Element(k)),
                             lambda j, i: (layer, pl.multiple_of(row0 + j * tn, SUBLANE), 0))
    return pl.pallas_call(
        _mm_t_kernel,
        grid=(n // tn, n_rows // tm),
        in_specs=[pl.BlockSpec((tm, k), lambda j, i: (i, 0)), wspec],
        out_specs=pl.BlockSpec((tm, tn), lambda j, i: (i, j)),
        out_shape=jax.ShapeDtypeStruct((n_rows, n), BF16),
        scratch_shapes=[pltpu.VMEM((tn, k), BF16)],
        compiler_params=_cp("arbitrary", "arbitrary"),
        name=name,
    )(x, w_t)


def _proj_kernel(ya_ref, yb_ref, yc_ref, wa_ref, wb_ref, wc_ref, ga_ref, gb_ref, gc_ref,
                 o_ref, wa_s, wb_s, wc_s):
    @pl.when(pl.program_id(1) == 0)
    def _():
        _cast_weight(wa_ref, wa_s)
        _cast_weight(wb_ref, wb_s)
        _cast_weight(wc_ref, wc_s)

    u = ga_ref[...].astype(F32) * jnp.dot(ya_ref[...], wa_s[...], preferred_element_type=F32)
    u = u + gb_ref[...].astype(F32) * jnp.dot(yb_ref[...], wb_s[...], preferred_element_type=F32)
    u = u + gc_ref[...].astype(F32) * jnp.dot(yc_ref[...], wc_s[...], preferred_element_type=F32)
    o_ref[...] = u.astype(o_ref.dtype)


def _proj_merge(ya, yb, yc, wa, wb, wc, layer, gates, n_rows):
    d = wa.shape[2]
    tn = _pick(d, TILE_CANDIDATES)
    tm = _row_tile(n_rows)
    nj = d // tn
    row = lambda j, i: (i, 0)
    col = lambda j, i: (layer, 0, j)
    return pl.pallas_call(
        _proj_kernel,
        grid=(nj, n_rows // tm),
        in_specs=[pl.BlockSpec((tm, ya.shape[1]), row),
                  pl.BlockSpec((tm, yb.shape[1]), row),
                  pl.BlockSpec((tm, yc.shape[1]), row),
                  pl.BlockSpec((1, wa.shape[1], tn), col),
                  pl.BlockSpec((1, wb.shape[1], tn), col),
                  pl.BlockSpec((1, wc.shape[1], tn), col),
                  pl.BlockSpec((tm, tn), lambda j, i: (i, j)),
                  pl.BlockSpec((tm, tn), lambda j, i: (i, nj + j)),
                  pl.BlockSpec((tm, tn), lambda j, i: (i, 2 * nj + j))],
        out_specs=pl.BlockSpec((tm, tn), lambda j, i: (i, j)),
        out_shape=jax.ShapeDtypeStruct((n_rows, d), BF16),
        scratch_shapes=[pltpu.VMEM((wa.shape[1], tn), BF16),
                        pltpu.VMEM((wb.shape[1], tn), BF16),
                        pltpu.VMEM((wc.shape[1], tn), BF16)],
        compiler_params=_cp("arbitrary", "arbitrary"),
        name="proj_merge",
    )(ya, yb, yc, wa, wb, wc, gates, gates, gates)


def _out_kernel(u_ref, w_ref, x_ref, g_ref, o_ref, wb_ref, *, bsz, seq):
    @pl.when(pl.program_id(1) == 0)
    def _():
        _cast_weight(w_ref, wb_ref)

    acc = jnp.dot(u_ref[...], wb_ref[...], preferred_element_type=F32)
    tm = acc.shape[0]
    row = pl.program_id(1) * tm + lax.broadcasted_iota(jnp.int32, (tm, 1), 0)
    g = g_ref[bsz:bsz + 1, :]
    for b in reversed(range(bsz)):
        g = jnp.where(row < (b + 1) * seq, g_ref[b:b + 1, :], g)
    o_ref[...] = x_ref[...] + g * acc


def _out_proj(u, w, layer, xall, gate, n_rows, bsz, seq):
    k, n = w.shape[1:]
    tn = _pick(n, TILE_CANDIDATES)
    tm = _row_tile(n_rows)
    return pl.pallas_call(
        functools.partial(_out_kernel, bsz=bsz, seq=seq),
        grid=(n // tn, n_rows // tm),
        in_specs=[pl.BlockSpec((tm, k), lambda j, i: (i, 0)),
                  pl.BlockSpec((1, k, tn), lambda j, i: (layer, 0, j)),
                  pl.BlockSpec((tm, tn), lambda j, i: (i, j)),
                  pl.BlockSpec((MOD_ROWS, tn), lambda j, i: (0, j))],
        out_specs=pl.BlockSpec((tm, tn), lambda j, i: (i, j)),
        out_shape=jax.ShapeDtypeStruct((n_rows, n), F32),
        scratch_shapes=[pltpu.VMEM((k, tn), BF16)],
        compiler_params=_cp("arbitrary", "arbitrary"),
        name="out_proj",
    )(u, w, xall, gate)


def _gla_prep_kernel(qk_ref, sm_ref, w2_ref, b2_ref, e_ref, pe_ref, cos_ref, sin_ref, tri_ref,
                     q_out, k_out, b_out):
    z = _dot_split(sm_ref[...], w2_ref[...]) + b2_ref[...]
    la = _log_sigmoid(z) * (1.0 / GLA_TAU)
    for d in range(2):
        rest = la[:, d * GLA_PAD:(d + 1) * GLA_PAD]
        acc = jnp.zeros(rest.shape, F32)
        for _ in range(3):
            part = rest.astype(BF16)
            acc = acc + jnp.dot(tri_ref[d], part, preferred_element_type=F32)
            rest = rest - part.astype(F32)
        b_out[d] = acc
    cos = cos_ref[...]
    sin = sin_ref[...]
    e = e_ref[...]
    pe = pe_ref[...]
    q = qk_ref[:, :A_QK]
    k = qk_ref[:, A_QK:]
    rope = lambda t: (jnp.dot(t, e, preferred_element_type=F32) * cos
                      + jnp.dot(t, pe, preferred_element_type=F32) * sin)
    q_out[...] = (rope(q) * (GLA_DK ** -0.5)).astype(BF16)
    k_out[...] = rope(k).astype(BF16)


def _gla_prep(p_a, small, w2pad, b2pad, e_mat, pe_mat, cos_t, sin_t, tr, pos_tile):
    r = p_a.shape[0]
    full = lambda i: (0, 0)
    idx = np.arange(tr)
    same_chunk = (idx[None, :] // GLA_CHUNK) == (idx[:, None] // GLA_CHUNK)
    tri = np.stack([same_chunk & (idx[None, :] <= idx[:, None]),
                    same_chunk & (idx[None, :] >= idx[:, None])]).astype(np.float32)
    tri = jnp.asarray(tri, BF16)
    return pl.pallas_call(
        _gla_prep_kernel,
        grid=(r // tr,),
        in_specs=[pl.BlockSpec((tr, 2 * A_QK), lambda i: (i, 0)),
                  pl.BlockSpec((tr, LANE), lambda i: (i, 0)),
                  pl.BlockSpec(w2pad.shape, full),
                  pl.BlockSpec(b2pad.shape, full),
                  pl.BlockSpec(e_mat.shape, full),
                  pl.BlockSpec(pe_mat.shape, full),
                  pl.BlockSpec((tr, GLA_PAD), lambda i: (pos_tile(i), 0)),
                  pl.BlockSpec((tr, GLA_PAD), lambda i: (pos_tile(i), 0)),
                  pl.BlockSpec(tri.shape, lambda i: (0, 0, 0))],
        out_specs=[pl.BlockSpec((tr, GLA_PAD), lambda i: (i, 0)),
                   pl.BlockSpec((tr, GLA_PAD), lambda i: (i, 0)),
                   pl.BlockSpec((2, tr, GLA_PAD), lambda i: (0, i, 0))],
        out_shape=[jax.ShapeDtypeStruct((r, GLA_PAD), BF16),
                   jax.ShapeDtypeStruct((r, GLA_PAD), BF16),
                   jax.ShapeDtypeStruct((2, r, GLA_PAD), F32)],
        compiler_params=_cp("arbitrary"),
        name="gla_prep",
    )(p_a, small, w2pad, b2pad, e_mat, pe_mat, cos_t, sin_t, tri)


def _gla_masks(rev):
    c = GLA_CHUNK
    idx = np.arange(c)
    pos = c - 1 - idx if rev else idx
    rb = pos // GLA_SUB
    rh = pos // (2 * GLA_SUB)
    md = (pos[None, :] <= pos[:, None]) & (rb[None, :] == rb[:, None])
    m16 = (rh[None, :] == rh[:, None]) & (rb[None, :] < rb[:, None])
    return np.stack([md, m16]).astype(np.float32)


def _gla_scan_kernel(qf_ref, kf_ref, vf_ref, bf_ref, qb_ref, kb_ref, vb_ref, bb_ref, mask_ref,
                     yf_ref, yb_ref, st_ref, dec_s, att_s, yi_s, *ops_s):
    sub = GLA_SUB
    nb = GLA_CHUNK // sub
    assert nb == 4

    @pl.when(pl.program_id(1) == 0)
    def _():
        st_ref[...] = jnp.zeros(st_ref.shape, F32)

    rows = lambda kb: slice(kb * sub, (kb + 1) * sub)
    cat = lambda parts: jnp.concatenate(parts, axis=0)
    zeros = jnp.zeros((sub, LANE), BF16)
    heads = [slice(h * LANE, (h + 1) * LANE) for h in range(GLA_HEADS)]
    qd_s, kd_s, k16_s, q32_s, k32_s, qi_s, ka_s = ops_s
    streams = [(0, False, qf_ref, kf_ref, vf_ref, bf_ref, yf_ref),
               (1, True, qb_ref, kb_ref, vb_ref, bb_ref, yb_ref)]
    for d, rev, q_ref, k_ref, v_ref, b_ref, y_ref in streams:
        rank = lambda kb: nb - 1 - kb if rev else kb
        by_rank = lambda rk: nb - 1 - rk if rev else rk
        for h, sl in enumerate(heads):
            b = b_ref[0, :, sl]
            q = q_ref[:, sl]
            k = k_ref[:, sl]

            def end(rk):
                kb = by_rank(rk)
                r0 = kb * sub if rev else kb * sub + sub - 1
                return b[r0:r0 + 1]

            tot = end(nb - 1)
            dq = cat([b[rows(kb)] - end(rank(kb) - 1) if rank(kb) > 0 else b[rows(kb)]
                      for kb in range(nb)])
            qd_s[d, h] = (q * jnp.exp(dq)).astype(BF16)
            kd_s[d, h] = (k * jnp.exp(-dq)).astype(BF16)
            k16_s[d, h] = cat([(k[rows(kb)] * jnp.exp(end(rank(kb)) - b[rows(kb)])).astype(BF16)
                               if rank(kb) % 2 == 0 else zeros for kb in range(nb)])
            half_end = end(nb // 2 - 1)
            q32_s[d, h] = cat([(q[rows(kb)] * jnp.exp(b[rows(kb)] - half_end)).astype(BF16)
                               if rank(kb) >= nb // 2 else zeros for kb in range(nb)])
            k32_s[d, h] = cat([(k[rows(kb)] * jnp.exp(half_end - b[rows(kb)])).astype(BF16)
                               if rank(kb) < nb // 2 else zeros for kb in range(nb)])
            qi_s[d, h] = (q * jnp.exp(b)).astype(BF16)
            ka_s[d, h] = (k * jnp.exp(tot - b)).astype(BF16)
            dec_s[d, h] = jnp.exp(tot)
    for d, rev, q_ref, k_ref, v_ref, b_ref, y_ref in streams:
        md = mask_ref[d, 0] > 0.0
        m16 = mask_ref[d, 1]
        for h, sl in enumerate(heads):
            qd = qd_s[d, h]
            att = (jnp.where(md, _nt(qd, kd_s[d, h]), 0.0) + m16 * _nt(qd, k16_s[d, h])
                   + _nt(q32_s[d, h], k32_s[d, h]))
            att_s[d, h] = att.astype(BF16)
            yi_s[d, h] = _nt(qi_s[d, h], st_ref[d, h].astype(BF16))
    for d, rev, q_ref, k_ref, v_ref, b_ref, y_ref in streams:
        for h, sl in enumerate(heads):
            v = v_ref[:, sl]
            y_ref[:, sl] = (jnp.dot(att_s[d, h], v, preferred_element_type=F32)
                            + yi_s[d, h]).astype(y_ref.dtype)
            st_ref[d, h] = st_ref[d, h] * dec_s[d, h] + _tn(v, ka_s[d, h])


def _chunk_index(bsz, n_lat, n_ctx, lat0, ctx0):
    def f(b, d, c):
        in_ctx = c < n_ctx
        cc = jnp.where(d == 0, c, n_ctx - 1 - c)
        cl = jnp.where(d == 0, c - n_ctx, n_lat - 1 - (c - n_ctx))
        return jnp.where(in_ctx, ctx0 + b * n_ctx + cc, lat0 + b * n_lat + cl)
    return f


def _gla_scan(qr, kr, p_a, bcum, bsz, seq, ctx_len):
    r = qr.shape[0]
    c = GLA_CHUNK
    masks = jnp.asarray(np.stack([_gla_masks(False), _gla_masks(True)]))
    n_lat, n_ctx = seq // c, ctx_len // c
    blk = _chunk_index(bsz, n_lat, n_ctx, 0, bsz * n_lat)
    v_col = (2 * A_QK) // A_V
    nh = GLA_HEADS

    def stream(d):
        return [pl.BlockSpec((c, GLA_PAD), lambda b, i: (blk(b, d, i), 0)),
                pl.BlockSpec((c, GLA_PAD), lambda b, i: (blk(b, d, i), 0)),
                pl.BlockSpec((c, A_V), lambda b, i: (blk(b, d, i), v_col)),
                pl.BlockSpec((1, c, GLA_PAD), lambda b, i: (d, blk(b, d, i), 0))]

    out = lambda d: pl.BlockSpec((c, A_V), lambda b, i: (blk(b, d, i), 0))
    return pl.pallas_call(
        _gla_scan_kernel,
        grid=(bsz, n_lat + n_ctx),
        in_specs=stream(0) + stream(1) + [pl.BlockSpec(masks.shape, lambda b, i: (0, 0, 0, 0))],
        out_specs=[out(0), out(1)],
        out_shape=[jax.ShapeDtypeStruct((r, A_V), BF16)] * 2,
        scratch_shapes=[pltpu.VMEM((2, nh, GLA_DV, LANE), F32),
                        pltpu.VMEM((2, nh, 1, LANE), F32),
                        pltpu.VMEM((2, nh, c, c), BF16),
                        pltpu.VMEM((2, nh, c, GLA_DV), F32)]
        + [pltpu.VMEM((2, nh, c, LANE), BF16)] * 7,
        compiler_params=_cp("arbitrary", "arbitrary"),
        name="gla_scan",
    )(qr, kr, p_a, bcum, qr, kr, p_a, bcum, masks)


def _ml_prep_kernel(x_ref, xp_ref, xn_ref, w_ref, b_ref, first_ref, o_ref, kt_ref):
    i = pl.program_id(0)
    x = x_ref[...].astype(F32)
    tr = x.shape[0]
    is_first = first_ref[i, 0] > 0
    is_last = first_ref[i, 1] > 0
    prev = jnp.where(is_first, 0.0, xp_ref[SUBLANE - 1:SUBLANE, :].astype(F32))
    nxt = jnp.where(is_last, 0.0, xn_ref[0:1, :].astype(F32))
    row = lax.broadcasted_iota(jnp.int32, x.shape, 0)
    xm1 = jnp.where(row == 0, prev, pltpu.roll(x, 1, axis=0))
    xp1 = jnp.where(row == tr - 1, nxt, pltpu.roll(x, tr - 1, axis=0))
    y = xm1 * w_ref[0:1, :] + x * w_ref[1:2, :] + xp1 * w_ref[2:3, :] + b_ref[...]
    y = _silu(y)
    k = y[:, B_W:] * (ML_DH ** -0.5)
    o_ref[:, :B_W] = y[:, :B_W].astype(BF16)
    o_ref[:, B_W:] = k.astype(BF16)
    kt_ref[...] = k.T.astype(BF16)


def _ml_prep(p_b, conv_w, conv_b, tr, edge_flags):
    r = p_b.shape[0]
    w = 2 * B_W
    nb = r // SUBLANE
    per_tile = tr // SUBLANE
    return pl.pallas_call(
        _ml_prep_kernel,
        grid=(r // tr,),
        in_specs=[pl.BlockSpec((tr, w), lambda i: (i, 0)),
                  pl.BlockSpec((SUBLANE, w), lambda i: (jnp.maximum(i * per_tile - 1, 0), 0)),
                  pl.BlockSpec((SUBLANE, w), lambda i: (jnp.minimum((i + 1) * per_tile, nb - 1), 0)),
                  pl.BlockSpec((SUBLANE, w), lambda i: (0, 0)),
                  pl.BlockSpec((1, w), lambda i: (0, 0)),
                  pl.BlockSpec(memory_space=pltpu.SMEM)],
        out_specs=[pl.BlockSpec((tr, w), lambda i: (i, 0)),
                   pl.BlockSpec((B_W, tr), lambda i: (0, i))],
        out_shape=[jax.ShapeDtypeStruct((r, w), BF16),
                   jax.ShapeDtypeStruct((B_W, r), BF16)],
        compiler_params=_cp("arbitrary"),
        name="mlstm_prep",
    )(p_b, p_b, p_b, jnp.pad(conv_w, ((0, SUBLANE - conv_w.shape[0]), (0, 0))), conv_b.reshape(1, w),
      edge_flags)


def _ml_scan_kernel(qkf_ref, ktf_ref, vf_ref, gif_ref, gff_ref, qkb_ref, ktb_ref, vb_ref, gib_ref,
                    gfb_ref, tri_ref, yf_ref, yb_ref, s_ref, m_ref, qk_s, qs_s, pk_s, wi_s, kw_s):
    c = ML_CHUNK
    nh = ML_HEADS
    assert c == ML_DH

    @pl.when(pl.program_id(1) == 0)
    def _():
        s_ref[...] = jnp.zeros(s_ref.shape, F32)
        m_ref[...] = jnp.zeros(m_ref.shape, F32)

    lane = lax.broadcasted_iota(jnp.int32, (nh, c), 1)
    ones = jnp.ones((c, ML_DH), BF16)
    heads = [slice(h * ML_DH, (h + 1) * ML_DH) for h in range(nh)]
    streams = [(0, qkf_ref, ktf_ref, vf_ref, gif_ref, gff_ref, yf_ref),
               (1, qkb_ref, ktb_ref, vb_ref, gib_ref, gfb_ref, yb_ref)]
    gates = []
    for d, qk_ref, kt_ref, v_ref, git_ref, gft_ref, y_ref in streams:
        tri = tri_ref[d]
        lft = _log_sigmoid(gft_ref[0])
        brow = lax.dot_general(lft, tri, (((1,), (1,)), ((), ())), precision=HI,
                               preferred_element_type=F32)
        tot = jnp.sum(lft, axis=1, keepdims=True)
        g = git_ref[0] - brow
        run = g
        sh = 1
        while sh < c:
            if d == 0:
                run = jnp.maximum(run, jnp.where(lane >= sh, pltpu.roll(run, sh, axis=1), NEG))
            else:
                run = jnp.maximum(run, jnp.where(lane < c - sh, pltpu.roll(run, c - sh, axis=1), NEG))
            sh *= 2
        m_prev = m_ref[d]
        big_m = jnp.maximum(m_prev, run)
        m_next = jnp.maximum(m_prev, jnp.max(g, axis=1, keepdims=True))
        decay = jnp.exp(m_prev - m_next)
        w_rows = jnp.exp(g - m_next)
        cols = jnp.concatenate([big_m, brow + big_m, jnp.zeros((c - 2 * nh, c), F32)], axis=0).T
        gates.append((g, m_prev, decay, w_rows, cols))
        m_ref[d] = tot + m_next
    for d, qk_ref, kt_ref, v_ref, git_ref, gft_ref, y_ref in streams:
        for h, sl in enumerate(heads):
            q = qk_ref[:, sl]
            qk_s[d, h] = jnp.dot(q, kt_ref[sl, :], preferred_element_type=F32)
            qs_s[d, h] = jnp.dot(q, s_ref[d, h].astype(BF16), preferred_element_type=F32)
    for d, qk_ref, kt_ref, v_ref, git_ref, gft_ref, y_ref in streams:
        g, m_prev, decay, w_rows, cols = gates[d]
        causal = tri_ref[d] > 0.0
        for h, sl in enumerate(heads):
            m_mat = jnp.broadcast_to(cols[:, h:h + 1], (c, c))
            p = jnp.where(causal, jnp.exp(g[h:h + 1, :] - m_mat), 0.0)
            pk_s[d, h] = (qk_s[d, h] * p).astype(BF16)
            wi_s[d, h] = jnp.exp(m_prev[h:h + 1, :] - m_mat)
            kw_s[d, h] = (kt_ref[sl, :].astype(F32) * w_rows[h:h + 1, :]).astype(BF16)
    for d, qk_ref, kt_ref, v_ref, git_ref, gft_ref, y_ref in streams:
        g, m_prev, decay, w_rows, cols = gates[d]
        for h, sl in enumerate(heads):
            vo = jnp.concatenate([v_ref[:, sl], ones], axis=1)
            w_inter = wi_s[d, h]
            nd = (jnp.dot(pk_s[d, h], vo, preferred_element_type=F32)
                  + jnp.concatenate([w_inter, w_inter], axis=1) * qs_s[d, h])
            neg_m = jnp.exp(-jnp.broadcast_to(cols[:, nh + h:nh + h + 1], (c, ML_DH)))
            y_ref[:, sl] = (nd[:, :ML_DH] / jnp.maximum(jnp.abs(nd[:, ML_DH:]), neg_m)).astype(y_ref.dtype)
            s_ref[d, h] = (decay[h:h + 1, :] * s_ref[d, h]
                           + jnp.dot(kw_s[d, h], vo, preferred_element_type=F32))


def _ml_scan(qk, kt, p_b, git, gft, bsz, seq, ctx_len):
    r = qk.shape[0]
    c = ML_CHUNK
    idx = np.arange(c)
    tri = np.stack([(idx[None, :] <= idx[:, None]), (idx[None, :] >= idx[:, None])]).astype(np.float32)
    n_lat, n_ctx = seq // c, ctx_len // c
    blk = _chunk_index(bsz, n_lat, n_ctx, 0, bsz * n_lat)
    h = ML_HEADS

    def stream(d):
        return [pl.BlockSpec((c, 2 * B_W), lambda b, i: (blk(b, d, i), 0)),
                pl.BlockSpec((B_W, c), lambda b, i: (0, blk(b, d, i))),
                pl.BlockSpec((c, B_W), lambda b, i: (blk(b, d, i), 2)),
                pl.BlockSpec((1, h, c), lambda b, i: (d, 0, blk(b, d, i))),
                pl.BlockSpec((1, h, c), lambda b, i: (d, 0, blk(b, d, i)))]

    out = lambda d: pl.BlockSpec((c, B_W), lambda b, i: (blk(b, d, i), 0))
    return pl.pallas_call(
        _ml_scan_kernel,
        grid=(bsz, n_lat + n_ctx),
        in_specs=stream(0) + stream(1) + [pl.BlockSpec((2, c, c), lambda b, i: (0, 0, 0))],
        out_specs=[out(0), out(1)],
        out_shape=[jax.ShapeDtypeStruct((r, B_W), BF16)] * 2,
        scratch_shapes=[pltpu.VMEM((2, h, ML_DH, 2 * ML_DH), F32),
                        pltpu.VMEM((2, h, 1), F32),
                        pltpu.VMEM((2, h, c, c), F32),
                        pltpu.VMEM((2, h, c, 2 * ML_DH), F32),
                        pltpu.VMEM((2, h, c, c), BF16),
                        pltpu.VMEM((2, h, c, c), F32),
                        pltpu.VMEM((2, h, ML_DH, c), BF16)],
        compiler_params=_cp("arbitrary", "arbitrary"),
        name="mlstm_scan",
    )(qk, kt, p_b, git, gft, qk, kt, p_b, git, gft, jnp.asarray(tri))


def _post_kernel(yf_ref, yb_ref, za_ref, g_ref, hf_ref, hb_ref, zb_ref, ob_ref, oa_ref, obb_ref):
    for h in range(GLA_HEADS):
        sl = slice(h * GLA_DV, (h + 1) * GLA_DV)
        o = yf_ref[:, sl].astype(F32) + yb_ref[:, sl].astype(F32)
        o = o * lax.rsqrt(jnp.mean(o * o, axis=1, keepdims=True) + EPS) * g_ref[:, sl]
        oa_ref[:, sl] = (o * _silu(za_ref[:, sl].astype(F32))).astype(BF16)
    hb = hf_ref[...].astype(F32) + hb_ref[...].astype(F32)
    obb_ref[...] = (jax.nn.sigmoid(ob_ref[...].astype(F32)) * hb
                    * _silu(zb_ref[...].astype(F32))).astype(BF16)


def _post(y_f, y_b, p_a, gnorm, h_f, h_b, p_b, tr):
    r = p_a.shape[0]
    za_col = (2 * A_QK + A_V) // A_V
    return pl.pallas_call(
        _post_kernel,
        grid=(r // tr,),
        in_specs=[pl.BlockSpec((tr, A_V), lambda i: (i, 0)),
                  pl.BlockSpec((tr, A_V), lambda i: (i, 0)),
                  pl.BlockSpec((tr, A_V), lambda i: (i, za_col)),
                  pl.BlockSpec((1, A_V), lambda i: (0, 0)),
                  pl.BlockSpec((tr, B_W), lambda i: (i, 0)),
                  pl.BlockSpec((tr, B_W), lambda i: (i, 0)),
                  pl.BlockSpec((tr, B_W), lambda i: (i, 3)),
                  pl.BlockSpec((tr, B_W), lambda i: (i, 4))],
        out_specs=[pl.BlockSpec((tr, A_V), lambda i: (i, 0)),
                   pl.BlockSpec((tr, B_W), lambda i: (i, 0))],
        out_shape=[jax.ShapeDtypeStruct((r, A_V), BF16),
                   jax.ShapeDtypeStruct((r, B_W), BF16)],
        compiler_params=_cp("arbitrary"),
        name="post_ab",
    )(y_f, y_b, p_a, gnorm.reshape(1, A_V), h_f, h_b, p_b, p_b)


def _na_kernel(q_ref, k_ref, v_ref, z_ref, qc_ref, kc_ref, vc_ref, zc_ref, t_ref, o_ref,
               s_ref, p_ref, *, rows, bsz, ctx_len, with_ctx):
    b = pl.program_id(0)
    scale = NA_DH ** -0.5
    if with_ctx:
        @pl.when(b == bsz)
        def _():
            for bb in range(bsz):
                rs_ = slice(bb * ctx_len, (bb + 1) * ctx_len)
                q = (qc_ref[rs_, :].astype(F32) * scale).astype(BF16)
                s = _nt(q, kc_ref[rs_, :])
                p = jnp.exp(s - jnp.max(s, axis=1, keepdims=True))
                o = (jnp.dot(p.astype(BF16), vc_ref[rs_, :], preferred_element_type=F32)
                     / jnp.sum(p, axis=1, keepdims=True))
                o_ref[rs_, :] = (o * _silu(zc_ref[rs_, :].astype(F32))).astype(BF16)
            o_ref[bsz * ctx_len:, :] = jnp.zeros((o_ref.shape[0] - bsz * ctx_len, NA_DH), BF16)

    @pl.when(b < bsz)
    def _():
        _na_latent(q_ref, k_ref, v_ref, z_ref, kc_ref, vc_ref, t_ref, o_ref, s_ref, p_ref,
                   rows=rows, c0=pl.multiple_of(b * ctx_len, ctx_len), ctx_len=ctx_len)


def _na_latent(q_ref, k_ref, v_ref, z_ref, kc_ref, vc_ref, t_ref, o_ref, s_ref, p_ref, *,
               rows, c0, ctx_len):
    kc = kc_ref[pl.ds(c0, ctx_len), :]
    vc = vc_ref[pl.ds(c0, ctx_len), :]
    win = NA_KH * GRID_W
    nq = NA_ROWS * GRID_W

    def body(t, carry):
        q0 = pl.multiple_of(t * nq, nq)
        q = (q_ref[pl.ds(q0, nq), :].astype(F32) * (NA_DH ** -0.5)).astype(BF16)
        starts = []
        for j in range(NA_ROWS):
            r = t * NA_ROWS + j
            rs = jnp.clip(r - NA_KH // 2, 0, rows - NA_KH)
            k0 = pl.multiple_of(rs * GRID_W, GRID_W)
            starts.append(k0)
            qs = slice(j * GRID_W, (j + 1) * GRID_W)
            s_ref[qs, :] = _nt(q[qs], k_ref[pl.ds(k0, win), :]) + t_ref[0, r - rs]
        s_c = _nt(q, kc)
        s_w = s_ref[...]
        m = jnp.maximum(jnp.max(s_w, axis=1, keepdims=True), jnp.max(s_c, axis=1, keepdims=True))
        p_w = jnp.exp(s_w - m)
        p_c = jnp.exp(s_c - m)
        l = jnp.sum(p_w, axis=1, keepdims=True) + jnp.sum(p_c, axis=1, keepdims=True)
        p_ref[...] = p_w.astype(BF16)
        o_c = jnp.dot(p_c.astype(BF16), vc, preferred_element_type=F32)
        o_w = jnp.concatenate(
            [jnp.dot(p_ref[j * GRID_W:(j + 1) * GRID_W, :], v_ref[pl.ds(starts[j], win), :],
                     preferred_element_type=F32) for j in range(NA_ROWS)], axis=0)
        z = z_ref[pl.ds(q0, nq), :].astype(F32)
        o_ref[pl.ds(q0, nq), :] = ((o_w + o_c) / l * _silu(z)).astype(BF16)
        return carry

    lax.fori_loop(0, rows // NA_ROWS, body, 0)


def _na_bias_table(rpb):
    c = np.arange(GRID_W)
    cs = np.clip(c - NA_KW // 2, 0, GRID_W - NA_KW)
    valid = (c[None, :] >= cs[:, None]) & (c[None, :] < cs[:, None] + NA_KW)
    dc = c[None, :] - c[:, None] + NA_KW - 1
    ndc = 2 * NA_KW - 1
    onehot = ((dc[None] == np.arange(ndc)[:, None, None]) & valid[None]).astype(np.float32)
    full = jnp.einsum("hdj,jck->hdck", rpb, jnp.asarray(onehot), precision=HI)
    full = full + jnp.asarray(np.where(valid, 0.0, NEG).astype(np.float32))
    t = jnp.stack([full[:, NA_KH - 1 - oi:2 * NA_KH - 1 - oi] for oi in range(NA_KH)], axis=1)
    t = jnp.transpose(t, (0, 1, 3, 2, 4))
    return t.reshape(NA_HEADS, NA_KH, GRID_W, NA_KH * GRID_W)


def _na(p_c, table, bsz, seq, ctx_len, with_ctx):
    rows = seq // GRID_W
    n_ctx = bsz * ctx_len
    assert (bsz * seq) % n_ctx == 0 and n_ctx <= seq
    ctx_blk = (bsz * seq) // n_ctx
    hq, hk, hv, hz = 0, NA_HEADS, 2 * NA_HEADS, 3 * NA_HEADS
    lat = lambda off: pl.BlockSpec((seq, NA_DH), lambda b, h: (jnp.minimum(b, bsz - 1), off + h))
    ctx = lambda off: pl.BlockSpec((n_ctx, NA_DH), lambda b, h: (ctx_blk, off + h))
    nb = bsz + int(with_ctx)
    return pl.pallas_call(
        functools.partial(_na_kernel, rows=rows, bsz=bsz, ctx_len=ctx_len, with_ctx=with_ctx),
        grid=(nb, NA_HEADS),
        in_specs=[lat(hq), lat(hk), lat(hv), lat(hz), ctx(hq), ctx(hk), ctx(hv), ctx(hz),
                  pl.BlockSpec((1,) + table.shape[1:], lambda b, h: (h, 0, 0, 0))],
        out_specs=pl.BlockSpec((seq, NA_DH), lambda b, h: (b, h)),
        out_shape=jax.ShapeDtypeStruct((nb * seq, C_W), BF16),
        scratch_shapes=[pltpu.VMEM((NA_ROWS * GRID_W, NA_KH * GRID_W), F32),
                        pltpu.VMEM((NA_ROWS * GRID_W, NA_KH * GRID_W), BF16)],
        compiler_params=_cp("arbitrary", "arbitrary"),
        name="nbr_attn",
    )(p_c, p_c, p_c, p_c, p_c, p_c, p_c, p_c, table)


def _rope_tables(seq, ctx_len):
    nf = GLA_DK // 4
    t = np.arange(seq)
    inv = ROPE_BASE ** (-np.arange(nf, dtype=np.float32) / nf)
    ar = (t // GRID_W).astype(np.float32)[:, None] * inv[None, :]
    ac = (t % GRID_W).astype(np.float32)[:, None] * inv[None, :]
    ar, ac = jnp.asarray(ar), jnp.asarray(ac)
    cos = jnp.concatenate([jnp.cos(ar), jnp.cos(ar), jnp.cos(ac), jnp.cos(ac)], axis=1)
    sin = jnp.concatenate([-jnp.sin(ar), jnp.sin(ar), -jnp.sin(ac), jnp.sin(ac)], axis=1)
    cos = jnp.concatenate([cos, jnp.ones((ctx_len, GLA_DK), F32)], axis=0)
    sin = jnp.concatenate([sin, jnp.zeros((ctx_len, GLA_DK), F32)], axis=0)
    pad = jnp.zeros((seq + ctx_len, LANE - GLA_DK), F32)
    cos = jnp.tile(jnp.concatenate([cos, pad], axis=1), (1, GLA_HEADS))
    sin = jnp.tile(jnp.concatenate([sin, pad], axis=1), (1, GLA_HEADS))
    return cos, sin


def _rope_mats():
    e = np.zeros((A_QK, GLA_PAD), np.float32)
    pe = np.zeros((A_QK, GLA_PAD), np.float32)
    quarter = GLA_DK // 4
    for i in range(A_QK):
        h, j = divmod(i, GLA_DK)
        e[i, h * LANE + j] = 1.0
    for h in range(GLA_HEADS):
        for j in range(GLA_DK):
            src = j + quarter if (j % (2 * quarter)) < quarter else j - quarter
            pe[h * GLA_DK + src, h * LANE + j] = 1.0
    return jnp.asarray(e, BF16), jnp.asarray(pe, BF16)


def _alpha_weights(w_alpha2, b_alpha):
    w = jnp.zeros((LANE, 2, GLA_HEADS, LANE), F32)
    b = jnp.zeros((2, GLA_HEADS, LANE), F32)
    for r in range(2):
        wr = w_alpha2[r].reshape(GLA_RANK, GLA_HEADS, GLA_DK)
        w = w.at[r * GLA_RANK:(r + 1) * GLA_RANK, r, :, :GLA_DK].set(wr)
        b = b.at[r, :, :GLA_DK].set(b_alpha[r].reshape(GLA_HEADS, GLA_DK))
    return w.reshape(LANE, 2 * GLA_PAD), b.reshape(1, 2 * GLA_PAD)


def kernel(x, c, ctx, c_ctx, w_mod, b_mod, norm_g, w_in, w_alpha2, b_alpha, gla_norm_g, b_gates,
           conv_w, conv_b, rpb, w_merge, b_merge, w_proj_a, w_proj_b, w_proj_c, w_out, final_g):
    bsz, seq, d = x.shape
    ctx_len = ctx.shape[1]
    depth = w_mod.shape[0]
    n_lat_rows = bsz * seq
    r_all = n_lat_rows + bsz * ctx_len
    tr = min(256, ctx_len)
    assert bsz < MOD_ROWS and seq % tr == 0 and ctx_len % tr == 0 and tr % ML_CHUNK == 0
    assert seq % (GRID_W * NA_ROWS) == 0 and seq // GRID_W >= NA_KH and n_lat_rows % ctx_len == 0

    lat_tiles = seq // tr
    group_tr = lambda i: jnp.minimum(i // lat_tiles, bsz)
    n_lat_t, ctx_tiles = n_lat_rows // tr, ctx_len // tr
    pos_tile = lambda i: jnp.where(i < n_lat_t, i % lat_tiles, lat_tiles + (i - n_lat_t) % ctx_tiles)
    ti = np.arange(r_all // tr)
    in_lat = ti < n_lat_t
    first = np.where(in_lat, ti % lat_tiles == 0, (ti - n_lat_t) % ctx_tiles == 0)
    last = np.where(in_lat, ti % lat_tiles == lat_tiles - 1, (ti - n_lat_t) % ctx_tiles == ctx_tiles - 1)
    edge_flags = jnp.asarray(np.stack([first, last], axis=1).astype(np.int32))

    cc = jnp.zeros((MOD_ROWS, d), F32).at[:bsz].set(c).at[bsz].set(c_ctx)
    mod = _modulation(cc, w_mod, b_mod)
    cos_t, sin_t = _rope_tables(seq, ctx_len)
    e_mat, pe_mat = _rope_mats()
    w_in_t = jnp.swapaxes(w_in, 1, 2)

    xall = jnp.concatenate([x.reshape(n_lat_rows, d), ctx.reshape(bsz * ctx_len, d)], axis=0)
    for l in range(depth):
        last_layer = l == depth - 1
        shift = mod[l, :, :d].reshape(MOD_ROWS, 1, d)
        scale = mod[l, :, d:2 * d].reshape(MOD_ROWS, 1, d)
        gate = mod[l, :, 2 * d:]
        w_small = jnp.zeros((LANE, d), F32)
        w_small = w_small.at[:2 * GLA_RANK].set(w_in_t[l, OFF_AA:OFF_B])
        w_small = w_small.at[2 * GLA_RANK:2 * GLA_RANK + 4 * ML_HEADS].set(w_in_t[l, OFF_GATES:OFF_C])
        h, small = _norm_mod(xall, norm_g[l], shift, scale, w_small.T, tr, group_tr)

        p_a = _mm_t(h, w_in_t, l, 0, OFF_AA, r_all, name="in_proj_a")
        p_b = _mm_t(h, w_in_t, l, OFF_B, 5 * B_W, r_all, name="in_proj_b")
        p_c = _mm_t(h, w_in_t, l, OFF_C, 4 * C_W, r_all, name="in_proj_c")

        w2pad, b2pad = _alpha_weights(w_alpha2[l], b_alpha[l])
        qr, kr, bcum = _gla_prep(p_a, small, w2pad, b2pad, e_mat, pe_mat, cos_t, sin_t, tr, pos_tile)
        y_f, y_b = _gla_scan(qr, kr, p_a, bcum, bsz, seq, ctx_len)

        qk_b, kt_b = _ml_prep(p_b, conv_w[l], conv_b[l], tr, edge_flags)
        g = small[:, 2 * GLA_RANK:2 * GLA_RANK + 4 * ML_HEADS].reshape(r_all, 4, ML_HEADS) + b_gates[l]
        g = jnp.transpose(g, (1, 2, 0))
        h_f, h_b = _ml_scan(qk_b, kt_b, p_b, g[0::2], g[1::2], bsz, seq, ctx_len)
        ya, yb = _post(y_f, y_b, p_a, gla_norm_g[l], h_f, h_b, p_b, tr)

        yc = _na(p_c, _na_bias_table(rpb[l]), bsz, seq, ctx_len, with_ctx=not last_layer)

        n_rows = n_lat_rows if last_layer else r_all
        gates = _mm_gate(h, w_merge, b_merge.reshape(depth, 1, -1), l, n_rows, name="merge_gates")
        u = _proj_merge(ya, yb, yc, w_proj_a, w_proj_b, w_proj_c, l, gates, n_rows)
        xall = _out_proj(u, w_out, l, xall, gate, n_rows, bsz, seq)

    out = _final_norm(xall, final_g, n_lat_rows, tr)
    return out.reshape(bsz, seq, d)
```
